```python
import jax, jax.numpy as jnp
from jax import lax
import numpy as np

D_MODEL = 1024
BATCH = 1
SEQ = 16384
DEPTH = 4

CHUNK = 64
SC_WIDTH = D_MODEL // 2
SC_KERNEL = 3
GLA_HEADS = 4
GLA_DK = D_MODEL // 2 // GLA_HEADS
GLA_DV = D_MODEL // GLA_HEADS
GLA_RANK = 16
GLA_GATE_NORM = 16.0
CF_KERNEL = 31
MLP_HIDDEN = 4 * D_MODEL
EPS = 1e-6

GLA_KEY = GLA_HEADS * GLA_DK
GLA_VAL = GLA_HEADS * GLA_DV
AB_SPLIT = (SC_WIDTH, SC_WIDTH, SC_WIDTH, GLA_KEY, GLA_KEY, GLA_VAL, GLA_VAL, GLA_RANK)
AB_IN = sum(AB_SPLIT)
AB_MIX = SC_WIDTH + GLA_VAL

kernel_name = "hybrid_shortconv_gla_conformer_trunk"


def rms_norm(x, g):
    xf = x.astype(jnp.float32)
    y = xf * lax.rsqrt(jnp.mean(jnp.square(xf), axis=-1, keepdims=True) + EPS)
    return (y * g.astype(jnp.float32)).astype(x.dtype)


def layer_norm(x, g, b):
    xf = x.astype(jnp.float32)
    mu = jnp.mean(xf, axis=-1, keepdims=True)
    xc = xf - mu
    y = xc * lax.rsqrt(jnp.mean(jnp.square(xc), axis=-1, keepdims=True) + EPS)
    return (y * g.astype(jnp.float32) + b.astype(jnp.float32)).astype(x.dtype)


def causal_depthwise_conv(x, w):
    K, C = w.shape
    return lax.conv_general_dilated(
        x, w[:, None, :].astype(x.dtype), window_strides=(1,), padding=[(K - 1, 0)],
        dimension_numbers=('NWC', 'WIO', 'NWC'), feature_group_count=C)


def split_cols(z, sizes):
    idx, acc = [], 0
    for s in sizes[:-1]:
        acc += s
        idx.append(acc)
    return jnp.split(z, idx, axis=-1)


def gla_chunked(q, k, v, gk, g_out, norm_g):
    B, T, _ = q.shape
    N = T // CHUNK
    out_dtype = v.dtype

    def heads(t, d):
        return t.reshape(B, N, CHUNK, GLA_HEADS, d).transpose(1, 0, 3, 2, 4).astype(jnp.float32)

    qh = heads(q, GLA_DK) * (GLA_DK ** -0.5)
    kh = heads(k, GLA_DK)
    vh = heads(v, GLA_DV)
    gh = heads(jax.nn.log_sigmoid(gk.astype(jnp.float32)) / GLA_GATE_NORM, GLA_DK)
    causal = jnp.tril(jnp.ones((CHUNK, CHUNK), dtype=bool))[None, None, :, :, None]

    def body(S, inp):
        qc, kc, vc, gc = inp
        b = jnp.cumsum(gc, axis=2)
        b_last = b[:, :, -1, :]
        o_inter = jnp.einsum('bhik,bhkv->bhiv', qc * jnp.exp(b), S)
        diff = b[:, :, :, None, :] - b[:, :, None, :, :]
        decay = jnp.where(causal, jnp.exp(jnp.minimum(diff, 0.0)), 0.0)
        attn = jnp.einsum('bhik,bhijk,bhjk->bhij', qc, decay, kc)
        o = o_inter + jnp.einsum('bhij,bhjv->bhiv', attn, vc)
        S_new = S * jnp.exp(b_last)[..., None] + jnp.einsum(
            'bhjk,bhjv->bhkv', kc * jnp.exp(b_last[:, :, None, :] - b), vc)
        return S_new, o

    S0 = jnp.zeros((B, GLA_HEADS, GLA_DK, GLA_DV), jnp.float32)
    _, o = lax.scan(body, S0, (qh, kh, vh, gh))
    o = o.transpose(1, 0, 3, 2, 4).reshape(B, T, GLA_HEADS, GLA_DV)
    o = rms_norm(o, norm_g) * jax.nn.silu(g_out.astype(jnp.float32).reshape(B, T, GLA_HEADS, GLA_DV))
    return o.reshape(B, T, GLA_VAL).astype(out_dtype)


def shortconv_gla_mixer(h, w_in, w_gk2, b_gk2, w_sc, gla_norm_g, w_out):
    z = h @ w_in
    sc_x, sc_b, sc_c, q, k, v, g_out, gk_lr = split_cols(z, AB_SPLIT)
    y_a = sc_b * causal_depthwise_conv(sc_c * sc_x, w_sc)
    gk = gk_lr @ w_gk2 + b_gk2
    y_b = gla_chunked(q, k, v, gk, g_out, gla_norm_g)
    return jnp.concatenate([y_a, y_b], axis=-1) @ w_out


def conformer_conv(h, w_pw1, b_pw1, w_dw, b_dw, ln_g, ln_b, w_pw2, b_pw2):
    u = h @ w_pw1 + b_pw1
    u1, u2 = jnp.split(u, 2, axis=-1)
    u = u1 * jax.nn.sigmoid(u2)
    d = causal_depthwise_conv(u, w_dw) + b_dw
    d = jax.nn.silu(layer_norm(d, ln_g, ln_b))
    return d @ w_pw2 + b_pw2


def setup_inputs(seed: int = 0) -> dict:
    key = jax.random.key(seed)
    ks = iter(jax.random.split(key, 32))
    ne = (DEPTH + 1) // 2
    no = DEPTH // 2
    D = D_MODEL

    def nrm(shape, scale):
        return jax.random.normal(next(ks), shape, jnp.float32) * scale

    def gain(shape):
        return 1.0 + nrm(shape, 0.02)

    return {
        "x": nrm((BATCH, SEQ, D), 1.0),
        "norm_mix_pre": gain((DEPTH, D)),
        "norm_mix_post": gain((DEPTH, D)),
        "norm_mlp_pre": gain((DEPTH, D)),
        "norm_mlp_post": gain((DEPTH, D)),
        "ab_w_in": nrm((ne, D, AB_IN), D ** -0.5),
        "gla_w_gk2": nrm((ne, GLA_RANK, GLA_KEY), GLA_RANK ** -0.5),
        "gla_b_gk2": nrm((ne, GLA_KEY), 0.01),
        "sc_w_conv": nrm((ne, SC_KERNEL, SC_WIDTH), SC_KERNEL ** -0.5),
        "gla_norm_g": gain((ne, GLA_DV)),
        "ab_w_out": nrm((ne, AB_MIX, D), AB_MIX ** -0.5),
        "cf_w_pw1": nrm((no, D, 2 * D), D ** -0.5),
        "cf_b_pw1": nrm((no, 2 * D), 0.01),
        "cf_w_dw": nrm((no, CF_KERNEL, D), CF_KERNEL ** -0.5),
        "cf_b_dw": nrm((no, D), 0.01),
        "cf_ln_g": gain((no, D)),
        "cf_ln_b": nrm((no, D), 0.01),
        "cf_w_pw2": nrm((no, D, D), D ** -0.5),
        "cf_b_pw2": nrm((no, D), 0.01),
        "mlp_w1": nrm((DEPTH, D, MLP_HIDDEN), D ** -0.5),
        "mlp_w2": nrm((DEPTH, MLP_HIDDEN, D), MLP_HIDDEN ** -0.5),
    }


def reference(x, norm_mix_pre, norm_mix_post, norm_mlp_pre, norm_mlp_post,
              ab_w_in, gla_w_gk2, gla_b_gk2, sc_w_conv, gla_norm_g, ab_w_out,
              cf_w_pw1, cf_b_pw1, cf_w_dw, cf_b_dw, cf_ln_g, cf_ln_b, cf_w_pw2, cf_b_pw2,
              mlp_w1, mlp_w2):
    for layer in range(DEPTH):
        i = layer // 2
        h = rms_norm(x, norm_mix_pre[layer])
        if layer % 2 == 0:
            y = shortconv_gla_mixer(h, ab_w_in[i], gla_w_gk2[i], gla_b_gk2[i],
                                    sc_w_conv[i], gla_norm_g[i], ab_w_out[i])
        else:
            y = conformer_conv(h, cf_w_pw1[i], cf_b_pw1[i], cf_w_dw[i], cf_b_dw[i],
                               cf_ln_g[i], cf_ln_b[i], cf_w_pw2[i], cf_b_pw2[i])
        x = x + rms_norm(y, norm_mix_post[layer])
        h = rms_norm(x, norm_mlp_pre[layer])
        y = jnp.square(jax.nn.relu(h @ mlp_w1[layer])) @ mlp_w2[layer]
        x = x + rms_norm(y, norm_mlp_post[layer])
    return x
```

```python
import functools

import jax
import jax.numpy as jnp
from jax import lax
from jax.experimental import pallas as pl
from jax.experimental.pallas import tpu as pltpu

F32 = jnp.float32
BF16 = jnp.bfloat16

EPS = 1e-6
GLA_HEADS = 4
GLA_RANK = 16
GLA_GATE_NORM = 16.0
SC_KERNEL = 3
CF_KERNEL = 31

LANES = 128
SUBLANES = 8
MXU_DIM = 256
VMEM_LIMIT_BYTES = 56 * 1024 * 1024

MLP_ROWS = 512
MLP_HIDDEN_CHUNK = 1024
EVEN_ROWS = 256
GLA_CHUNK = 128
ODD_ROWS = 256
CONV_ROW_BLOCK = 32
CF_HALO = 32


def _rms(x, g):
    ms = jnp.mean(x * x, axis=-1, keepdims=True)
    return x * lax.rsqrt(ms + EPS) * g


def _const_spec(shape, index_map):
    return pl.BlockSpec(shape, index_map, pipeline_mode=pl.Buffered(1))


def _mlp_kernel(x_ref, gpre_ref, gpost_ref, w1_ref, w2_ref, o_ref, *, hidden_chunk):
    x = x_ref[...]
    h = _rms(x, gpre_ref[...]).astype(BF16)
    hidden = w1_ref.shape[1]
    acc = None
    for c in range(hidden // hidden_chunk):
        cols = slice(c * hidden_chunk, (c + 1) * hidden_chunk)
        a = jnp.dot(h, w1_ref[:, cols], preferred_element_type=F32)
        a = jnp.maximum(a, 0.0)
        a = (a * a).astype(BF16)
        p = jnp.dot(a, w2_ref[cols, :], preferred_element_type=F32)
        acc = p if acc is None else acc + p
    o_ref[...] = x + _rms(acc, gpost_ref[...])


def _mlp_call(x2d, g_pre, g_post, w1, w2, layer):
    rows, d = x2d.shape
    hidden = w1.shape[-1]
    tm = MLP_ROWS
    assert rows % tm == 0 and hidden % MLP_HIDDEN_CHUNK == 0
    return pl.pallas_call(
        functools.partial(_mlp_kernel, hidden_chunk=MLP_HIDDEN_CHUNK),
        grid=(rows // tm,),
        in_specs=[
            pl.BlockSpec((tm, d), lambda i: (i, 0)),
            _const_spec((None, 1, d), lambda i: (layer, 0, 0)),
            _const_spec((None, 1, d), lambda i: (layer, 0, 0)),
            _const_spec((None, d, hidden), lambda i: (layer, 0, 0)),
            _const_spec((None, hidden, d), lambda i: (layer, 0, 0)),
        ],
        out_specs=pl.BlockSpec((tm, d), lambda i: (i, 0)),
        out_shape=jax.ShapeDtypeStruct((rows, d), F32),
        compiler_params=pltpu.CompilerParams(
            dimension_semantics=("arbitrary",), vmem_limit_bytes=VMEM_LIMIT_BYTES),
        name="mlp_block",
    )(x2d, g_pre, g_post, w1, w2)


def _even_kernel(x_ref, gpre_ref, gpost_ref, win_ref, wgk2_ref, bgk2_ref, wsc_ref,
                 gng_ref, wout_ref, o_ref, z_ref, mix_ref, ubuf_ref, s_ref,
                 *, rows, chunk, d_model):
    sc = d_model // 2
    dk = d_model // 2 // GLA_HEADS
    dv = d_model // GLA_HEADS
    key = GLA_HEADS * dk
    o_scx, o_scb, o_scc = 0, sc, 2 * sc
    o_q = 3 * sc
    o_k = o_q + key
    o_v = o_k + key
    o_g = o_v + GLA_HEADS * dv
    o_lr = o_g + GLA_HEADS * dv

    t = pl.program_id(1)

    @pl.when(t == 0)
    def _():
        s_ref[...] = jnp.zeros_like(s_ref)
        ubuf_ref[0:SUBLANES, :] = jnp.zeros((SUBLANES, sc), F32)

    x = x_ref[...]
    h = _rms(x, gpre_ref[...]).astype(BF16)
    z_ref[...] = jnp.dot(h, win_ref[...], preferred_element_type=F32)

    u = z_ref[:, o_scc:o_scc + sc] * z_ref[:, o_scx:o_scx + sc]
    ubuf_ref[SUBLANES:SUBLANES + rows, :] = u
    conv = wsc_ref[SC_KERNEL - 1:SC_KERNEL, :] * u
    for j in range(SC_KERNEL - 1):
        shift = SC_KERNEL - 1 - j
        conv = conv + wsc_ref[j:j + 1, :] * ubuf_ref[SUBLANES - shift:SUBLANES - shift + rows, :]
    ubuf_ref[0:SUBLANES, :] = u[rows - SUBLANES:rows, :]
    mix_ref[:, 0:sc] = (z_ref[:, o_scb:o_scb + sc] * conv).astype(BF16)

    gk_lr = z_ref[:, o_lr:o_lr + LANES].astype(BF16)
    gk = jnp.dot(gk_lr, wgk2_ref[...], preferred_element_type=F32) + bgk2_ref[...]
    g = (jnp.minimum(gk, 0.0) - jnp.log1p(jnp.exp(-jnp.abs(gk)))) * (1.0 / GLA_GATE_NORM)
    g_hi = g.astype(BF16)
    g_lo = (g - g_hi.astype(F32)).astype(BF16)
    ri = lax.broadcasted_iota(jnp.int32, (rows, rows), 0)
    ci = lax.broadcasted_iota(jnp.int32, (rows, rows), 1)
    same_chunk = (ri // chunk) == (ci // chunk)
    tril_chunks = jnp.where(same_chunk & (ci <= ri), 1.0, 0.0).astype(BF16)
    b = (jnp.dot(tril_chunks, g_hi, preferred_element_type=F32)
         + jnp.dot(tril_chunks, g_lo, preferred_element_type=F32))

    causal = (lax.broadcasted_iota(jnp.int32, (chunk, chunk), 1)
              <= lax.broadcasted_iota(jnp.int32, (chunk, chunk), 0))
    scale = dk ** -0.5
    nt_dims = (((1,), (1,)), ((), ()))
    tn_dims = (((0,), (0,)), ((), ()))
    gng = gng_ref[...]

    for hd in range(GLA_HEADS):
        state = s_ref[hd]
        for c in range(rows // chunk):
            r0 = c * chunk
            bc = b[r0:r0 + chunk, hd * dk:(hd + 1) * dk]
            qc = z_ref[r0:r0 + chunk, o_q + hd * dk:o_q + (hd + 1) * dk]
            kc = z_ref[r0:r0 + chunk, o_k + hd * dk:o_k + (hd + 1) * dk]
            vc = z_ref[r0:r0 + chunk, o_v + hd * dv:o_v + (hd + 1) * dv].astype(BF16)
            gout = z_ref[r0:r0 + chunk, o_g + hd * dv:o_g + (hd + 1) * dv]
            b_last = bc[chunk - 1:chunk, :]
            b_mid = bc[chunk // 2 - 1:chunk // 2, :]
            q_in = qc * jnp.exp(bc) * scale
            q_mid = (q_in * jnp.exp(-b_mid)).astype(BF16)
            k_mid = kc * jnp.exp(b_mid - bc)
            k_end = (k_mid * jnp.exp(b_last - b_mid)).astype(BF16)
            attn = lax.dot_general(q_mid, k_mid.astype(BF16), nt_dims,
                                   preferred_element_type=F32)
            attn = jnp.where(causal, attn, 0.0).astype(BF16)
            o = (jnp.dot(q_in.astype(BF16), state.astype(BF16), preferred_element_type=F32)
                 + jnp.dot(attn, vc, preferred_element_type=F32))
            decay = jnp.broadcast_to(jnp.exp(b_last), (dk, dk)).T
            state = (state * jnp.concatenate([decay] * (dv // dk), axis=1)
                     + lax.dot_general(k_end, vc, tn_dims, preferred_element_type=F32))
            on = o * lax.rsqrt(jnp.mean(o * o, axis=-1, keepdims=True) + EPS) * gng
            yb = on * (gout * jax.nn.sigmoid(gout))
            mix_ref[r0:r0 + chunk, sc + hd * dv:sc + (hd + 1) * dv] = yb.astype(BF16)
        s_ref[hd] = state

    y = jnp.dot(mix_ref[...], wout_ref[...], preferred_element_type=F32)
    o_ref[...] = x + _rms(y, gpost_ref[...])


def _even_call(x3d, g_pre, g_post, w_in, w_gk2, b_gk2, w_sc, gn_g, w_out, layer, idx):
    bsz, seq, d = x3d.shape
    rows, chunk = EVEN_ROWS, GLA_CHUNK
    assert seq % rows == 0 and rows % chunk == 0
    n_in = w_in.shape[-1]
    key = w_gk2.shape[-1]
    sc = w_sc.shape[-1]
    mix = w_out.shape[1]
    dk = key // GLA_HEADS
    dv = d // GLA_HEADS
    kern = functools.partial(_even_kernel, rows=rows, chunk=chunk, d_model=d)
    return pl.pallas_call(
        kern,
        grid=(bsz, seq // rows),
        in_specs=[
            pl.BlockSpec((None, rows, d), lambda b, i: (b, i, 0)),
            _const_spec((None, 1, d), lambda b, i: (layer, 0, 0)),
            _const_spec((None, 1, d), lambda b, i: (layer, 0, 0)),
            _const_spec((None, d, n_in), lambda b, i: (idx, 0, 0)),
            _const_spec((None, LANES, key), lambda b, i: (idx, 0, 0)),
            _const_spec((None, 1, key), lambda b, i: (idx, 0, 0)),
            _const_spec((None, SC_KERNEL, sc), lambda b, i: (idx, 0, 0)),
            _const_spec((None, 1, dv), lambda b, i: (idx, 0, 0)),
            _const_spec((None, mix, d), lambda b, i: (idx, 0, 0)),
        ],
        out_specs=pl.BlockSpec((None, rows, d), lambda b, i: (b, i, 0)),
        out_shape=jax.ShapeDtypeStruct((bsz, seq, d), F32),
        scratch_shapes=[
            pltpu.VMEM((rows, n_in), F32),
            pltpu.VMEM((rows, mix), BF16),
            pltpu.VMEM((rows + SUBLANES, sc), F32),
            pltpu.VMEM((GLA_HEADS, dk, dv), F32),
        ],
        compiler_params=pltpu.CompilerParams(
            dimension_semantics=("arbitrary", "arbitrary"),
            vmem_limit_bytes=VMEM_LIMIT_BYTES),
        name="shortconv_gla_block",
    )(x3d, g_pre, g_post, w_in, w_gk2, b_gk2, w_sc, gn_g, w_out)


def _odd_kernel(x_ref, gpre_ref, gpost_ref, wpw1_ref, bpw1_ref, wdw_ref, bdw_ref,
                lng_ref, lnb_ref, wpw2_ref, bpw2_ref, o_ref, ubuf_ref, dbuf_ref,
                *, rows, d_model):
    t = pl.program_id(1)

    @pl.when(t == 0)
    def _():
        ubuf_ref[0:CF_HALO, :] = jnp.zeros((CF_HALO, d_model), F32)

    x = x_ref[...]
    h = _rms(x, gpre_ref[...]).astype(BF16)
    u = jnp.dot(h, wpw1_ref[...], preferred_element_type=F32) + bpw1_ref[...]
    ubuf_ref[CF_HALO:CF_HALO + rows, :] = u[:, :d_model] * jax.nn.sigmoid(u[:, d_model:])

    first = CF_HALO - (CF_KERNEL - 1)
    rb = CONV_ROW_BLOCK
    for lg in range(d_model // LANES):
        lanes = slice(lg * LANES, (lg + 1) * LANES)
        taps = [wdw_ref[j:j + 1, lanes] for j in range(CF_KERNEL)]
        bias = bdw_ref[:, lanes]
        for r in range(rows // rb):
            acc = bias + taps[0] * ubuf_ref[first + r * rb:first + (r + 1) * rb, lanes]
            for j in range(1, CF_KERNEL):
                acc = acc + taps[j] * ubuf_ref[first + j + r * rb:first + j + (r + 1) * rb, lanes]
            dbuf_ref[r * rb:(r + 1) * rb, lanes] = acc
    ubuf_ref[0:CF_HALO, :] = ubuf_ref[rows:rows + CF_HALO, :]

    d = dbuf_ref[...]
    mu = jnp.mean(d, axis=-1, keepdims=True)
    dc = d - mu
    dn = dc * lax.rsqrt(jnp.mean(dc * dc, axis=-1, keepdims=True) + EPS)
    dn = dn * lng_ref[...] + lnb_ref[...]
    act = (dn * jax.nn.sigmoid(dn)).astype(BF16)
    y = jnp.dot(act, wpw2_ref[...], preferred_element_type=F32) + bpw2_ref[...]
    o_ref[...] = x + _rms(y, gpost_ref[...])


def _odd_call(x3d, g_pre, g_post, w_pw1, b_pw1, w_dw, b_dw, ln_g, ln_b, w_pw2, b_pw2,
              layer, idx):
    bsz, seq, d = x3d.shape
    rows = ODD_ROWS
    assert seq % rows == 0 and rows % CONV_ROW_BLOCK == 0 and rows >= CF_HALO
    kern = functools.partial(_odd_kernel, rows=rows, d_model=d)
    return pl.pallas_call(
        kern,
        grid=(bsz, seq // rows),
        in_specs=[
            pl.BlockSpec((None, rows, d), lambda b, i: (b, i, 0)),
            _const_spec((None, 1, d), lambda b, i: (layer, 0, 0)),
            _const_spec((None, 1, d), lambda b, i: (layer, 0, 0)),
            _const_spec((None, d, 2 * d), lambda b, i: (idx, 0, 0)),
            _const_spec((None, 1, 2 * d), lambda b, i: (idx, 0, 0)),
            _const_spec((None, CF_KERNEL, d), lambda b, i: (idx, 0, 0)),
            _const_spec((None, 1, d), lambda b, i: (idx, 0, 0)),
            _const_spec((None, 1, d), lambda b, i: (idx, 0, 0)),
            _const_spec((None, 1, d), lambda b, i: (idx, 0, 0)),
            _const_spec((None, d, d), lambda b, i: (idx, 0, 0)),
            _const_spec((None, 1, d), lambda b, i: (idx, 0, 0)),
        ],
        out_specs=pl.BlockSpec((None, rows, d), lambda b, i: (b, i, 0)),
        out_shape=jax.ShapeDtypeStruct((bsz, seq, d), F32),
        scratch_shapes=[
            pltpu.VMEM((rows + CF_HALO, d), F32),
            pltpu.VMEM((rows, d), F32),
        ],
        compiler_params=pltpu.CompilerParams(
            dimension_semantics=("arbitrary", "arbitrary"),
            vmem_limit_bytes=VMEM_LIMIT_BYTES),
        name="conformer_conv_block",
    )(x3d, g_pre, g_post, w_pw1, b_pw1, w_dw, b_dw, ln_g, ln_b, w_pw2, b_pw2)


def kernel(x, norm_mix_pre, norm_mix_post, norm_mlp_pre, norm_mlp_post,
           ab_w_in, gla_w_gk2, gla_b_gk2, sc_w_conv, gla_norm_g, ab_w_out,
           cf_w_pw1, cf_b_pw1, cf_w_dw, cf_b_dw, cf_ln_g, cf_ln_b, cf_w_pw2, cf_b_pw2,
           mlp_w1, mlp_w2):
    bsz, seq, d = x.shape
    depth = norm_mix_pre.shape[0]
    n_in = ab_w_in.shape[-1]
    n_main = n_in - GLA_RANK
    assert n_main % LANES == 0

    n_pad = -(-(n_main + LANES) // MXU_DIM) * MXU_DIM
    w_in = jnp.pad(ab_w_in.astype(BF16), ((0, 0), (0, 0), (0, n_pad - n_in)))
    w_gk2 = jnp.pad(gla_w_gk2.astype(BF16), ((0, 0), (0, LANES - GLA_RANK), (0, 0)))
    w_out = ab_w_out.astype(BF16)
    w_pw1 = cf_w_pw1.astype(BF16)
    w_pw2 = cf_w_pw2.astype(BF16)
    w1 = mlp_w1.astype(BF16)
    w2 = mlp_w2.astype(BF16)

    def rows3d(v):
        return v.reshape(v.shape[0], 1, v.shape[1])

    norm_mix_pre, norm_mix_post, norm_mlp_pre, norm_mlp_post = map(
        rows3d, (norm_mix_pre, norm_mix_post, norm_mlp_pre, norm_mlp_post))
    gla_b_gk2, gla_norm_g = rows3d(gla_b_gk2), rows3d(gla_norm_g)
    cf_b_pw1, cf_b_dw, cf_ln_g, cf_ln_b, cf_b_pw2 = map(
        rows3d, (cf_b_pw1, cf_b_dw, cf_ln_g, cf_ln_b, cf_b_pw2))

    for layer in range(depth):
        idx = layer // 2
        if layer % 2 == 0:
            x = _even_call(x, norm_mix_pre, norm_mix_post, w_in, w_gk2, gla_b_gk2,
                           sc_w_conv, gla_norm_g, w_out, layer, idx)
        else:
            x = _odd_call(x, norm_mix_pre, norm_mix_post, w_pw1, cf_b_pw1, cf_w_dw,
                          cf_b_dw, cf_ln_g, cf_ln_b, w_pw2, cf_b_pw2, layer, idx)
        x = _mlp_call(x.reshape(bsz * seq, d), norm_mlp_pre, norm_mlp_post, w1, w2,
                      layer).reshape(bsz, seq, d)
    return x
```

```python
import functools

import jax
import jax.numpy as jnp
from jax import lax
from jax.experimental import pallas as pl
from jax.experimental.pallas import tpu as pltpu

F32 = jnp.float32
BF16 = jnp.bfloat16

EPS = 1e-6
GLA_HEADS = 4
GLA_RANK = 16
GLA_GATE_NORM = 16.0
SC_KERNEL = 3
CF_KERNEL = 31

LANES = 128
SUBLANES = 8
MXU_DIM = 256
VMEM_LIMIT_BYTES = 56 * 1024 * 1024
SCHEDULER_FLAGS = None

TILE_ROWS = 256
MLP_HIDDEN_CHUNK = 1024
GLA_CHUNK = 128
CF_HALO = 32


def _rms(x, g):
    ms = jnp.mean(x * x, axis=-1, keepdims=True)
    return x * lax.rsqrt(ms + EPS) * g


def _const_spec(shape, index_map):
    return pl.BlockSpec(shape, index_map, pipeline_mode=pl.Buffered(1))


def _zero_bits_of(v):
    u = lax.bitcast_convert_type(v, jnp.uint32)
    return lax.shift_right_logical(lax.shift_right_logical(u, jnp.uint32(16)), jnp.uint32(16))


def _order_after(ref, block, token):
    bits = lax.bitcast_convert_type(ref[block], jnp.uint32) | token
    ref[block] = lax.bitcast_convert_type(bits, F32)


def _mlp_stage(x1buf_ref, gpre_ref, gpost_ref, w1_ref, w2_ref, o_ref, done):
    h = _rms(x1buf_ref[...], gpre_ref[...]).astype(BF16)
    hidden = w1_ref.shape[1]
    acc = None
    for c in range(hidden // MLP_HIDDEN_CHUNK):
        cols = slice(c * MLP_HIDDEN_CHUNK, (c + 1) * MLP_HIDDEN_CHUNK)
        yield
        a = jnp.dot(h, w1_ref[:, cols], preferred_element_type=F32)
        a = jnp.maximum(a, 0.0)
        a = (a * a).astype(BF16)
        yield
        p = jnp.dot(a, w2_ref[cols, :], preferred_element_type=F32)
        acc = p if acc is None else acc + p
    yield
    out = x1buf_ref[...] + _rms(acc, gpost_ref[...])
    o_ref[...] = out
    done.append(_zero_bits_of(out[0:SUBLANES, 0:LANES]))


def _interleave(*gens):
    live = list(gens)
    while live:
        for g in list(live):
            try:
                next(g)
            except StopIteration:
                live.remove(g)


def _even_mixer_stage(x_ref, gpre_ref, win_ref, wgk2_ref, bgk2_ref, wsc_ref, gng_ref, wout_ref,
                      z_ref, mix_ref, ubuf_ref, s_ref, result, *, rows, chunk, d_model):
    sc = d_model // 2
    dk = d_model // 2 // GLA_HEADS
    dv = d_model // GLA_HEADS
    key = GLA_HEADS * dk
    o_scx, o_scb, o_scc = 0, sc, 2 * sc
    o_q = 3 * sc
    o_k = o_q + key
    o_v = o_k + key
    o_g = o_v + GLA_HEADS * dv
    o_lr = o_g + GLA_HEADS * dv

    h = _rms(x_ref[...], gpre_ref[...]).astype(BF16)
    z_ref[...] = jnp.dot(h, win_ref[...], preferred_element_type=F32)
    yield

    gk_lr = z_ref[:, o_lr:o_lr + LANES].astype(BF16)
    gk = jnp.dot(gk_lr, wgk2_ref[...], preferred_element_type=F32) + bgk2_ref[...]
    yield
    g = (jnp.minimum(gk, 0.0) - jnp.log1p(jnp.exp(-jnp.abs(gk)))) * (1.0 / GLA_GATE_NORM)
    g_hi = g.astype(BF16)
    g_lo = (g - g_hi.astype(F32)).astype(BF16)
    ri = lax.broadcasted_iota(jnp.int32, (rows, rows), 0)
    ci = lax.broadcasted_iota(jnp.int32, (rows, rows), 1)
    same_chunk = (ri // chunk) == (ci // chunk)
    tril_chunks = jnp.where(same_chunk & (ci <= ri), 1.0, 0.0).astype(BF16)
    b = (jnp.dot(tril_chunks, g_hi, preferred_element_type=F32)
         + jnp.dot(tril_chunks, g_lo, preferred_element_type=F32))
    yield

    u = z_ref[:, o_scc:o_scc + sc] * z_ref[:, o_scx:o_scx + sc]
    ubuf_ref[SUBLANES:SUBLANES + rows, :] = u
    conv = wsc_ref[SC_KERNEL - 1:SC_KERNEL, :] * u
    for j in range(SC_KERNEL - 1):
        shift = SC_KERNEL - 1 - j
        conv = conv + wsc_ref[j:j + 1, :] * ubuf_ref[SUBLANES - shift:SUBLANES - shift + rows, :]
    ubuf_ref[0:SUBLANES, :] = u[rows - SUBLANES:rows, :]
    mix_ref[:, 0:sc] = (z_ref[:, o_scb:o_scb + sc] * conv).astype(BF16)

    causal = (lax.broadcasted_iota(jnp.int32, (chunk, chunk), 1)
              <= lax.broadcasted_iota(jnp.int32, (chunk, chunk), 0))
    scale = dk ** -0.5
    nt_dims = (((1,), (1,)), ((), ()))
    tn_dims = (((0,), (0,)), ((), ()))
    gng = gng_ref[...]
    n_chunks = rows // chunk
    pairs = [(hd, c) for hd in range(GLA_HEADS) for c in range(n_chunks)]

    q_in, v_bf, decay, attn, kv = {}, {}, {}, {}, {}
    for hd, c in pairs:
        r0 = c * chunk
        bc = b[r0:r0 + chunk, hd * dk:(hd + 1) * dk]
        qc = z_ref[r0:r0 + chunk, o_q + hd * dk:o_q + (hd + 1) * dk]
        kc = z_ref[r0:r0 + chunk, o_k + hd * dk:o_k + (hd + 1) * dk]
        v_bf[hd, c] = z_ref[r0:r0 + chunk, o_v + hd * dv:o_v + (hd + 1) * dv].astype(BF16)
        b_last = bc[chunk - 1:chunk, :]
        b_mid = bc[chunk // 2 - 1:chunk // 2, :]
        qd = qc * jnp.exp(bc) * scale
        q_mid = (qd * jnp.exp(-b_mid)).astype(BF16)
        k_mid = kc * jnp.exp(b_mid - bc)
        k_end = (k_mid * jnp.exp(b_last - b_mid)).astype(BF16)
        q_in[hd, c] = qd.astype(BF16)
        decay[hd, c] = jnp.broadcast_to(jnp.exp(b_last), (dk, dk)).T
        attn[hd, c] = lax.dot_general(q_mid, k_mid.astype(BF16), nt_dims,
                                      preferred_element_type=F32)
        kv[hd, c] = lax.dot_general(k_end, v_bf[hd, c], tn_dims, preferred_element_type=F32)
    yield

    outs = {}
    for hd in range(GLA_HEADS):
        state = s_ref[hd]
        for c in range(n_chunks):
            a = jnp.where(causal, attn[hd, c], 0.0).astype(BF16)
            outs[hd, c] = (jnp.dot(q_in[hd, c], state.astype(BF16), preferred_element_type=F32)
                           + jnp.dot(a, v_bf[hd, c], preferred_element_type=F32))
            state = (state * jnp.concatenate([decay[hd, c]] * (dv // dk), axis=1) + kv[hd, c])
        s_ref[hd] = state
    yield

    for hd, c in pairs:
        r0 = c * chunk
        o = outs[hd, c]
        gout = z_ref[r0:r0 + chunk, o_g + hd * dv:o_g + (hd + 1) * dv]
        on = o * lax.rsqrt(jnp.mean(o * o, axis=-1, keepdims=True) + EPS) * gng
        yb = on * (gout * jax.nn.sigmoid(gout))
        mix_ref[r0:r0 + chunk, sc + hd * dv:sc + (hd + 1) * dv] = yb.astype(BF16)
    result.append(jnp.dot(mix_ref[...], wout_ref[...], preferred_element_type=F32))


def _even_kernel(x_ref, gpre_ref, gpost_ref, win_ref, wgk2_ref, bgk2_ref, wsc_ref, gng_ref,
                 wout_ref, mgpre_ref, mgpost_ref, w1_ref, w2_ref, o_ref,
                 x1buf_ref, z_ref, mix_ref, ubuf_ref, s_ref, *, rows, chunk, d_model):
    @pl.when(pl.program_id(1) == 0)
    def _():
        x1buf_ref[...] = jnp.zeros_like(x1buf_ref)
        s_ref[...] = jnp.zeros_like(s_ref)
        ubuf_ref[0:SUBLANES, :] = jnp.zeros((SUBLANES, ubuf_ref.shape[1]), F32)

    mixed, mlp_done = [], []
    _interleave(
        _even_mixer_stage(x_ref, gpre_ref, win_ref, wgk2_ref, bgk2_ref, wsc_ref, gng_ref,
                          wout_ref, z_ref, mix_ref, ubuf_ref, s_ref, mixed,
                          rows=rows, chunk=chunk, d_model=d_model),
        _mlp_stage(x1buf_ref, mgpre_ref, mgpost_ref, w1_ref, w2_ref, o_ref, mlp_done))
    x1buf_ref[...] = x_ref[...] + _rms(mixed[0], gpost_ref[...])


def _tile_specs(rows, d, n_tiles):
    x_spec = pl.BlockSpec((None, rows, d), lambda b, s: (b, jnp.minimum(s, n_tiles - 1), 0))
    o_spec = pl.BlockSpec((None, rows, d), lambda b, s: (b, jnp.maximum(s - 1, 0), 0))
    return x_spec, o_spec


def _even_call(x3d, g_pre, g_post, w_in, w_gk2, b_gk2, w_sc, gn_g, w_out,
               mg_pre, mg_post, w1, w2, layer, idx):
    bsz, seq, d = x3d.shape
    rows, chunk = TILE_ROWS, GLA_CHUNK
    assert seq % rows == 0 and rows % chunk == 0
    n_tiles = seq // rows
    n_in = w_in.shape[-1]
    key = w_gk2.shape[-1]
    sc = w_sc.shape[-1]
    mix = w_out.shape[1]
    hidden = w1.shape[-1]
    dk = key // GLA_HEADS
    dv = d // GLA_HEADS
    x_spec, o_spec = _tile_specs(rows, d, n_tiles)
    kern = functools.partial(_even_kernel, rows=rows, chunk=chunk, d_model=d)
    return pl.pallas_call(
        kern,
        grid=(bsz, n_tiles + 1),
        in_specs=[
            x_spec,
            _const_spec((None, 1, d), lambda b, s: (layer, 0, 0)),
            _const_spec((None, 1, d), lambda b, s: (layer, 0, 0)),
            _const_spec((None, d, n_in), lambda b, s: (idx, 0, 0)),
            _const_spec((None, LANES, key), lambda b, s: (idx, 0, 0)),
            _const_spec((None, 1, key), lambda b, s: (idx, 0, 0)),
            _const_spec((None, SC_KERNEL, sc), lambda b, s: (idx, 0, 0)),
            _const_spec((None, 1, dv), lambda b, s: (idx, 0, 0)),
            _const_spec((None, mix, d), lambda b, s: (idx, 0, 0)),
            _const_spec((None, 1, d), lambda b, s: (layer, 0, 0)),
            _const_spec((None, 1, d), lambda b, s: (layer, 0, 0)),
            _const_spec((None, d, hidden), lambda b, s: (layer, 0, 0)),
            _const_spec((None, hidden, d), lambda b, s: (layer, 0, 0)),
        ],
        out_specs=o_spec,
        out_shape=jax.ShapeDtypeStruct((bsz, seq, d), F32),
        scratch_shapes=[
            pltpu.VMEM((rows, d), F32),
            pltpu.VMEM((rows, n_in), F32),
            pltpu.VMEM((rows, mix), BF16),
            pltpu.VMEM((rows + SUBLANES, sc), F32),
            pltpu.VMEM((GLA_HEADS, dk, dv), F32),
        ],
        compiler_params=pltpu.CompilerParams(
            dimension_semantics=("arbitrary", "arbitrary"),
            vmem_limit_bytes=VMEM_LIMIT_BYTES, flags=SCHEDULER_FLAGS),
        name="shortconv_gla_layer",
    )(x3d, g_pre, g_post, w_in, w_gk2, b_gk2, w_sc, gn_g, w_out, mg_pre, mg_post, w1, w2)


def _depthwise_conv(ubuf_ref, wdw_ref, bdw_ref, dbuf_ref, *, rows, d_model):
    sub = SUBLANES
    halo_blocks = CF_HALO // sub
    n_blocks = rows // sub
    max_a = (CF_KERNEL - 1) // sub
    assert max_a < halo_blocks
    row_id = lax.broadcasted_iota(jnp.int32, (sub, LANES), 0)
    for lg in range(d_model // LANES):
        lanes = slice(lg * LANES, (lg + 1) * LANES)
        w = {}
        for s in range(CF_KERNEL):
            w[s] = jnp.broadcast_to(wdw_ref[CF_KERNEL - 1 - s:CF_KERNEL - s, lanes], (sub, LANES))
        bias = jnp.broadcast_to(bdw_ref[:, lanes], (sub, LANES))
        blocks = {}

        def block(m):
            if m not in blocks:
                blocks[m] = ubuf_ref[m * sub:(m + 1) * sub, lanes]
            return blocks[m]

        prev_rolled = None
        for m in range(halo_blocks - 1, halo_blocks + n_blocks):
            q = []
            for r in range(sub):
                acc = None
                for a in range(max_a + 1):
                    s = sub * a + r
                    if s >= CF_KERNEL:
                        continue
                    term = w[s] * block(m - a)
                    acc = term if acc is None else acc + term
                q.append(acc)
            rolled = [None] + [pltpu.roll(q[r], r, axis=0) for r in range(1, sub)]
            if m >= halo_blocks:
                out = bias + q[0]
                for r in range(1, sub):
                    out = out + jnp.where(row_id >= r, rolled[r], prev_rolled[r])
                k = m - halo_blocks
                dbuf_ref[k * sub:(k + 1) * sub, lanes] = out
            prev_rolled = rolled
            blocks.pop(m - max_a, None)
        yield


def _odd_kernel(x_ref, gpre_ref, gpost_ref, wpw1_ref, bpw1_ref, wdw_ref, bdw_ref,
                lng_ref, lnb_ref, wpw2_ref, bpw2_ref, mgpre_ref, mgpost_ref, w1_ref, w2_ref,
                o_ref, x1buf_ref, ubuf_ref, dbuf_ref, *, rows, d_model):
    @pl.when(pl.program_id(1) == 0)
    def _():
        x1buf_ref[...] = jnp.zeros_like(x1buf_ref)
        ubuf_ref[0:CF_HALO, :] = jnp.zeros((CF_HALO, d_model), F32)

    x = x_ref[...]
    h = _rms(x, gpre_ref[...]).astype(BF16)
    u = jnp.dot(h, wpw1_ref[...], preferred_element_type=F32) + bpw1_ref[...]
    ubuf_ref[CF_HALO:CF_HALO + rows, :] = u[:, :d_model] * jax.nn.sigmoid(u[:, d_model:])

    mlp_done = []
    _interleave(
        _mlp_stage(x1buf_ref, mgpre_ref, mgpost_ref, w1_ref, w2_ref, o_ref, mlp_done),
        _depthwise_conv(ubuf_ref, wdw_ref, bdw_ref, dbuf_ref, rows=rows, d_model=d_model))
    ubuf_ref[0:CF_HALO, :] = ubuf_ref[rows:rows + CF_HALO, :]

    _order_after(dbuf_ref, (slice(0, SUBLANES), slice(0, LANES)), mlp_done[0])
    d = dbuf_ref[...]
    mu = jnp.mean(d, axis=-1, keepdims=True)
    dc = d - mu
    dn = dc * lax.rsqrt(jnp.mean(dc * dc, axis=-1, keepdims=True) + EPS)
    dn = dn * lng_ref[...] + lnb_ref[...]
    act = (dn * jax.nn.sigmoid(dn)).astype(BF16)
    y = jnp.dot(act, wpw2_ref[...], preferred_element_type=F32) + bpw2_ref[...]
    x1buf_ref[...] = x_ref[...] + _rms(y, gpost_ref[...])


def _odd_call(x3d, g_pre, g_post, w_pw1, b_pw1, w_dw, b_dw, ln_g, ln_b, w_pw2, b_pw2,
              mg_pre, mg_post, w1, w2, layer, idx):
    bsz, seq, d = x3d.shape
    rows = TILE_ROWS
    assert seq % rows == 0 and rows % SUBLANES == 0 and rows >= CF_HALO
    n_tiles = seq // rows
    hidden = w1.shape[-1]
    x_spec, o_spec = _tile_specs(rows, d, n_tiles)
    kern = functools.partial(_odd_kernel, rows=rows, d_model=d)
    return pl.pallas_call(
        kern,
        grid=(bsz, n_tiles + 1),
        in_specs=[
            x_spec,
            _const_spec((None, 1, d), lambda b, s: (layer, 0, 0)),
            _const_spec((None, 1, d), lambda b, s: (layer, 0, 0)),
            _const_spec((None, d, 2 * d), lambda b, s: (idx, 0, 0)),
            _const_spec((None, 1, 2 * d), lambda b, s: (idx, 0, 0)),
            _const_spec((None, CF_KERNEL, d), lambda b, s: (idx, 0, 0)),
            _const_spec((None, 1, d), lambda b, s: (idx, 0, 0)),
            _const_spec((None, 1, d), lambda b, s: (idx, 0, 0)),
            _const_spec((None, 1, d), lambda b, s: (idx, 0, 0)),
            _const_spec((None, d, d), lambda b, s: (idx, 0, 0)),
            _const_spec((None, 1, d), lambda b, s: (idx, 0, 0)),
            _const_spec((None, 1, d), lambda b, s: (layer, 0, 0)),
            _const_spec((None, 1, d), lambda b, s: (layer, 0, 0)),
            _const_spec((None, d, hidden), lambda b, s: (layer, 0, 0)),
            _const_spec((None, hidden, d), lambda b, s: (layer, 0, 0)),
        ],
        out_specs=o_spec,
        out_shape=jax.ShapeDtypeStruct((bsz, seq, d), F32),
        scratch_shapes=[
            pltpu.VMEM((rows, d), F32),
            pltpu.VMEM((rows + CF_HALO, d), F32),
            pltpu.VMEM((rows, d), F32),
        ],
        compiler_params=pltpu.CompilerParams(
            dimension_semantics=("arbitrary", "arbitrary"),
            vmem_limit_bytes=VMEM_LIMIT_BYTES, flags=SCHEDULER_FLAGS),
        name="conformer_layer",
    )(x3d, g_pre, g_post, w_pw1, b_pw1, w_dw, b_dw, ln_g, ln_b, w_pw2, b_pw2,
      mg_pre, mg_post, w1, w2)


def kernel(x, norm_mix_pre, norm_mix_post, norm_mlp_pre, norm_mlp_post,
           ab_w_in, gla_w_gk2, gla_b_gk2, sc_w_conv, gla_norm_g, ab_w_out,
           cf_w_pw1, cf_b_pw1, cf_w_dw, cf_b_dw, cf_ln_g, cf_ln_b, cf_w_pw2, cf_b_pw2,
           mlp_w1, mlp_w2):
    depth = norm_mix_pre.shape[0]
    n_in = ab_w_in.shape[-1]
    n_main = n_in - GLA_RANK
    assert n_main % LANES == 0 and mlp_w1.shape[-1] % MLP_HIDDEN_CHUNK == 0

    n_pad = -(-(n_main + LANES) // MXU_DIM) * MXU_DIM
    w_in = jnp.pad(ab_w_in.astype(BF16), ((0, 0), (0, 0), (0, n_pad - n_in)))
    w_gk2 = jnp.pad(gla_w_gk2.astype(BF16), ((0, 0), (0, LANES - GLA_RANK), (0, 0)))
    w_out = ab_w_out.astype(BF16)
    w_pw1 = cf_w_pw1.astype(BF16)
    w_pw2 = cf_w_pw2.astype(BF16)
    w1 = mlp_w1.astype(BF16)
    w2 = mlp_w2.astype(BF16)

    def rows3d(v):
        return v.reshape(v.shape[0], 1, v.shape[1])

    norm_mix_pre, norm_mix_post, norm_mlp_pre, norm_mlp_post = map(
        rows3d, (norm_mix_pre, norm_mix_post, norm_mlp_pre, norm_mlp_post))
    gla_b_gk2, gla_norm_g = rows3d(gla_b_gk2), rows3d(gla_norm_g)
    cf_b_pw1, cf_b_dw, cf_ln_g, cf_ln_b, cf_b_pw2 = map(
        rows3d, (cf_b_pw1, cf_b_dw, cf_ln_g, cf_ln_b, cf_b_pw2))

    for layer in range(depth):
        idx = layer // 2
        if layer % 2 == 0:
            x = _even_call(x, norm_mix_pre, norm_mix_post, w_in, w_gk2, gla_b_gk2,
                           sc_w_conv, gla_norm_g, w_out, norm_mlp_pre, norm_mlp_post,
                           w1, w2, layer, idx)
        else:
            x = _odd_call(x, norm_mix_pre, norm_mix_post, w_pw1, cf_b_pw1, cf_w_dw,
                          cf_b_dw, cf_ln_g, cf_ln_b, w_pw2, cf_b_pw2, norm_mlp_pre,
                          norm_mlp_post, w1, w2, layer, idx)
    return x
```

```python
import functools

import jax
import jax.numpy as jnp
from jax import lax
from jax.experimental import pallas as pl
from jax.experimental.pallas import tpu as pltpu

F32 = jnp.float32
BF16 = jnp.bfloat16

EPS = 1e-6
GLA_HEADS = 4
GLA_RANK = 16
GLA_GATE_NORM = 16.0
SC_KERNEL = 3
CF_KERNEL = 31

LANES = 128
SUBLANES = 8
MXU_DIM = 256
VMEM_LIMIT_BYTES = 56 * 1024 * 1024

TILE_ROWS = 256
MLP_HIDDEN_CHUNK = 1024
GLA_CHUNK = 128
GLA_FAST_PATH_MAX_DECAY = 60.0
CF_HALO = 32

def _rms(x, g):
    ms = jnp.mean(x * x, axis=-1, keepdims=True)
    return x * lax.rsqrt(ms + EPS) * g


def _const_spec(shape, index_map):
    return pl.BlockSpec(shape, index_map, pipeline_mode=pl.Buffered(1))


def _zero_bits_of(v):
    u = lax.bitcast_convert_type(v, jnp.uint32)
    return lax.shift_right_logical(lax.shift_right_logical(u, jnp.uint32(16)), jnp.uint32(16))


def _order_after(ref, block, token):
    bits = lax.bitcast_convert_type(ref[block], jnp.uint32) | token
    ref[block] = lax.bitcast_convert_type(bits, F32)


def _store_mixer_output(x1, mgpre_ref, x1buf_ref, hbuf_ref):
    x1buf_ref[...] = x1
    hbuf_ref[...] = _rms(x1, mgpre_ref[...]).astype(BF16)


def _mlp_stage(x1buf_ref, h, gpost_ref, w1_ref, w2_ref, o_ref, done, *, tail_parts):
    rows = h.shape[0]
    part = rows // tail_parts
    hidden = w1_ref.shape[1]
    n_chunks = hidden // MLP_HIDDEN_CHUNK
    acc = None
    for c in range(n_chunks):
        cols = slice(c * MLP_HIDDEN_CHUNK, (c + 1) * MLP_HIDDEN_CHUNK)
        a = jnp.dot(h, w1_ref[:, cols], preferred_element_type=F32)
        a = jnp.maximum(a, 0.0)
        a = (a * a).astype(BF16)
        yield
        if c < n_chunks - 1:
            p = jnp.dot(a, w2_ref[cols, :], preferred_element_type=F32)
            acc = p if acc is None else acc + p
            yield
    out = None
    for t in range(tail_parts):
        sl = slice(t * part, (t + 1) * part)
        p = jnp.dot(a[sl], w2_ref[cols, :], preferred_element_type=F32)
        total = p if acc is None else acc[sl] + p
        out = x1buf_ref[sl, :] + _rms(total, gpost_ref[...])
        o_ref[sl, :] = out
    done.append(_zero_bits_of(out[0:SUBLANES, 0:LANES]))


def _interleave(*gens):
    live = list(gens)
    while live:
        for g in list(live):
            try:
                next(g)
            except StopIteration:
                live.remove(g)


def _even_mixer_stage(x_ref, gpre_ref, win_ref, wgk2_ref, bgk2_ref, wsc_ref, gng_ref, wout_ref,
                      z_ref, mix_ref, ubuf_ref, s_ref, bbuf_ref, sprev_ref, result,
                      *, rows, chunk, d_model):
    sc = d_model // 2
    dk = d_model // 2 // GLA_HEADS
    dv = d_model // GLA_HEADS
    key = GLA_HEADS * dk
    o_scx, o_scb, o_scc = 0, sc, 2 * sc
    o_q = 3 * sc
    o_k = o_q + key
    o_v = o_k + key
    o_g = o_v + GLA_HEADS * dv
    o_lr = o_g + GLA_HEADS * dv

    h = _rms(x_ref[...], gpre_ref[...]).astype(BF16)
    z_ref[...] = jnp.dot(h, win_ref[...], preferred_element_type=F32)
    yield

    gk_lr = z_ref[:, o_lr:o_lr + LANES].astype(BF16)
    gk = jnp.dot(gk_lr, wgk2_ref[...], preferred_element_type=F32) + bgk2_ref[...]
    yield
    g = (jnp.minimum(gk, 0.0) - jnp.log1p(jnp.exp(-jnp.abs(gk)))) * (1.0 / GLA_GATE_NORM)
    g_hi = g.astype(BF16)
    g_lo = (g - g_hi.astype(F32)).astype(BF16)
    ri = lax.broadcasted_iota(jnp.int32, (rows, rows), 0)
    ci = lax.broadcasted_iota(jnp.int32, (rows, rows), 1)
    same_chunk = (ri // chunk) == (ci // chunk)
    tril_chunks = jnp.where(same_chunk & (ci <= ri), 1.0, 0.0).astype(BF16)
    b = (jnp.dot(tril_chunks, g_hi, preferred_element_type=F32)
         + jnp.dot(tril_chunks, g_lo, preferred_element_type=F32))
    bbuf_ref[...] = b
    sprev_ref[...] = s_ref[...]
    n_chunks = rows // chunk
    chunk_totals = jnp.concatenate(
        [b[(c + 1) * chunk - 1:(c + 1) * chunk, :] for c in range(n_chunks)], axis=0)
    result.append(jnp.min(chunk_totals))
    yield

    u = z_ref[:, o_scc:o_scc + sc] * z_ref[:, o_scx:o_scx + sc]
    ubuf_ref[SUBLANES:SUBLANES + rows, :] = u
    conv = wsc_ref[SC_KERNEL - 1:SC_KERNEL, :] * u
    for j in range(SC_KERNEL - 1):
        shift = SC_KERNEL - 1 - j
        conv = conv + wsc_ref[j:j + 1, :] * ubuf_ref[SUBLANES - shift:SUBLANES - shift + rows, :]
    ubuf_ref[0:SUBLANES, :] = u[rows - SUBLANES:rows, :]
    mix_ref[:, 0:sc] = (z_ref[:, o_scb:o_scb + sc] * conv).astype(BF16)

    causal = (lax.broadcasted_iota(jnp.int32, (chunk, chunk), 1)
              <= lax.broadcasted_iota(jnp.int32, (chunk, chunk), 0))
    scale = dk ** -0.5
    nt_dims = (((1,), (1,)), ((), ()))
    tn_dims = (((0,), (0,)), ((), ()))
    gng = gng_ref[...]
    pairs = [(hd, c) for hd in range(GLA_HEADS) for c in range(n_chunks)]

    q_in, v_bf, decay, attn, kv = {}, {}, {}, {}, {}
    for hd, c in pairs:
        r0 = c * chunk
        bc = b[r0:r0 + chunk, hd * dk:(hd + 1) * dk]
        qc = z_ref[r0:r0 + chunk, o_q + hd * dk:o_q + (hd + 1) * dk]
        kc = z_ref[r0:r0 + chunk, o_k + hd * dk:o_k + (hd + 1) * dk]
        v_bf[hd, c] = z_ref[r0:r0 + chunk, o_v + hd * dv:o_v + (hd + 1) * dv].astype(BF16)
        b_last = bc[chunk - 1:chunk, :]
        b_mid = bc[chunk // 2 - 1:chunk // 2, :]
        qd = qc * jnp.exp(bc) * scale
        q_mid = (qd * jnp.exp(-b_mid)).astype(BF16)
        k_mid = kc * jnp.exp(b_mid - bc)
        k_end = (kc * jnp.exp(b_last - bc)).astype(BF16)
        q_in[hd, c] = qd.astype(BF16)
        decay[hd, c] = jnp.broadcast_to(jnp.exp(b_last), (dk, dk)).T
        attn[hd, c] = lax.dot_general(q_mid, k_mid.astype(BF16), nt_dims,
                                      preferred_element_type=F32)
        kv[hd, c] = lax.dot_general(k_end, v_bf[hd, c], tn_dims, preferred_element_type=F32)
    yield

    outs = {}
    for hd in range(GLA_HEADS):
        state = s_ref[hd]
        for c in range(n_chunks):
            a = jnp.where(causal, attn[hd, c], 0.0).astype(BF16)
            outs[hd, c] = (jnp.dot(q_in[hd, c], state.astype(BF16), preferred_element_type=F32)
                           + jnp.dot(a, v_bf[hd, c], preferred_element_type=F32))
            state = (state * jnp.concatenate([decay[hd, c]] * (dv // dk), axis=1) + kv[hd, c])
        s_ref[hd] = state
    yield

    for hd, c in pairs:
        r0 = c * chunk
        o = outs[hd, c]
        gout = z_ref[r0:r0 + chunk, o_g + hd * dv:o_g + (hd + 1) * dv]
        on = o * lax.rsqrt(jnp.mean(o * o, axis=-1, keepdims=True) + EPS) * gng
        yb = on * (gout * jax.nn.sigmoid(gout))
        mix_ref[r0:r0 + chunk, sc + hd * dv:sc + (hd + 1) * dv] = yb.astype(BF16)
    result.append(jnp.dot(mix_ref[...], wout_ref[...], preferred_element_type=F32))


def _even_exact_tile(x_ref, gpost_ref, gng_ref, wout_ref, mgpre_ref, x1buf_ref, hbuf_ref,
                     z_ref, mix_ref, bbuf_ref, sprev_ref, orow_ref, *, rows, chunk, d_model):
    sc = d_model // 2
    dk = d_model // 2 // GLA_HEADS
    dv = d_model // GLA_HEADS
    key = GLA_HEADS * dk
    o_q = 3 * sc
    o_k = o_q + key
    o_v = o_k + key
    o_g = o_v + GLA_HEADS * dv
    scale = dk ** -0.5
    tn_dims = (((0,), (0,)), ((), ()))
    gng = gng_ref[...]
    row_id = lax.broadcasted_iota(jnp.int32, (chunk, 1), 0)
    for hd in range(GLA_HEADS):
        klanes = slice(hd * dk, (hd + 1) * dk)
        qlanes = slice(o_q + hd * dk, o_q + (hd + 1) * dk)
        state = sprev_ref[hd]
        for c in range(rows // chunk):
            r0 = c * chunk
            bc = bbuf_ref[r0:r0 + chunk, klanes]
            kc = z_ref[r0:r0 + chunk, o_k + hd * dk:o_k + (hd + 1) * dk]
            vc = z_ref[r0:r0 + chunk, o_v + hd * dv:o_v + (hd + 1) * dv]
            gout = z_ref[r0:r0 + chunk, o_g + hd * dv:o_g + (hd + 1) * dv]
            b_last = bc[chunk - 1:chunk, :]
            q_in = (z_ref[r0:r0 + chunk, qlanes] * jnp.exp(bc) * scale).astype(BF16)
            k_end = (kc * jnp.exp(b_last - bc)).astype(BF16)

            def eight_rows(blk, carry):
                base = pl.multiple_of(blk * SUBLANES, SUBLANES)
                b_blk = bbuf_ref[pl.ds(r0 + base, SUBLANES), klanes]
                q_blk = z_ref[pl.ds(r0 + base, SUBLANES), qlanes] * scale
                out_rows = []
                for r in range(SUBLANES):
                    decay_ij = jnp.exp(jnp.minimum(b_blk[r:r + 1, :] - bc, 0.0))
                    w = (q_blk[r:r + 1, :] * decay_ij) * kc
                    a = jnp.where(row_id <= base + r, jnp.sum(w, axis=-1, keepdims=True), 0.0)
                    out_rows.append(jnp.sum(a * vc, axis=0, keepdims=True))
                orow_ref[pl.ds(base, SUBLANES), :] = jnp.concatenate(out_rows, axis=0)
                return carry

            lax.fori_loop(0, chunk // SUBLANES, eight_rows, 0)
            o = (jnp.dot(q_in, state.astype(BF16), preferred_element_type=F32) + orow_ref[...])
            decay = jnp.broadcast_to(jnp.exp(b_last), (dk, dk)).T
            state = (state * jnp.concatenate([decay] * (dv // dk), axis=1)
                     + lax.dot_general(k_end, vc.astype(BF16), tn_dims,
                                       preferred_element_type=F32))
            on = o * lax.rsqrt(jnp.mean(o * o, axis=-1, keepdims=True) + EPS) * gng
            yb = on * (gout * jax.nn.sigmoid(gout))
            mix_ref[r0:r0 + chunk, sc + hd * dv:sc + (hd + 1) * dv] = yb.astype(BF16)
    y = jnp.dot(mix_ref[...], wout_ref[...], preferred_element_type=F32)
    _store_mixer_output(x_ref[...] + _rms(y, gpost_ref[...]), mgpre_ref, x1buf_ref, hbuf_ref)


def _even_kernel(x_ref, gpre_ref, gpost_ref, win_ref, wgk2_ref, bgk2_ref, wsc_ref, gng_ref,
                 wout_ref, mgpre_ref, mgpost_ref, w1_ref, w2_ref, o_ref,
                 x1buf_ref, hbuf_ref, z_ref, mix_ref, ubuf_ref, s_ref, bbuf_ref, sprev_ref,
                 orow_ref, *, rows, chunk, d_model):
    @pl.when(pl.program_id(1) == 0)
    def _():
        x1buf_ref[...] = jnp.zeros_like(x1buf_ref)
        hbuf_ref[...] = jnp.zeros_like(hbuf_ref)
        s_ref[...] = jnp.zeros_like(s_ref)
        ubuf_ref[0:SUBLANES, :] = jnp.zeros((SUBLANES, ubuf_ref.shape[1]), F32)

    mixed, mlp_done = [], []
    _interleave(
        _mlp_stage(x1buf_ref, hbuf_ref[...], mgpost_ref, w1_ref, w2_ref, o_ref, mlp_done,
                   tail_parts=2),
        _even_mixer_stage(x_ref, gpre_ref, win_ref, wgk2_ref, bgk2_ref, wsc_ref, gng_ref,
                          wout_ref, z_ref, mix_ref, ubuf_ref, s_ref, bbuf_ref, sprev_ref, mixed,
                          rows=rows, chunk=chunk, d_model=d_model))
    min_chunk_decay, y = mixed
    _store_mixer_output(x_ref[...] + _rms(y, gpost_ref[...]), mgpre_ref, x1buf_ref, hbuf_ref)

    @pl.when(min_chunk_decay < -GLA_FAST_PATH_MAX_DECAY)
    def _():
        _even_exact_tile(x_ref, gpost_ref, gng_ref, wout_ref, mgpre_ref, x1buf_ref, hbuf_ref,
                         z_ref, mix_ref, bbuf_ref, sprev_ref, orow_ref,
                         rows=rows, chunk=chunk, d_model=d_model)


def _tile_specs(rows, d, n_tiles):
    x_spec = pl.BlockSpec((None, rows, d), lambda b, s: (b, jnp.minimum(s, n_tiles - 1), 0))
    o_spec = pl.BlockSpec((None, rows, d), lambda b, s: (b, jnp.maximum(s - 1, 0), 0))
    return x_spec, o_spec


def _even_call(x3d, g_pre, g_post, w_in, w_gk2, b_gk2, w_sc, gn_g, w_out,
               mg_pre, mg_post, w1, w2, layer, idx):
    bsz, seq, d = x3d.shape
    rows, chunk = TILE_ROWS, GLA_CHUNK
    assert seq % rows == 0 and rows % chunk == 0
    n_tiles = seq // rows
    n_in = w_in.shape[-1]
    key = w_gk2.shape[-1]
    sc = w_sc.shape[-1]
    mix = w_out.shape[1]
    hidden = w1.shape[-1]
    dk = key // GLA_HEADS
    dv = d // GLA_HEADS
    x_spec, o_spec = _tile_specs(rows, d, n_tiles)
    kern = functools.partial(_even_kernel, rows=rows, chunk=chunk, d_model=d)
    return pl.pallas_call(
        kern,
        grid=(bsz, n_tiles + 1),
        in_specs=[
            x_spec,
            _const_spec((None, 1, d), lambda b, s: (layer, 0, 0)),
            _const_spec((None, 1, d), lambda b, s: (layer, 0, 0)),
            _const_spec((None, d, n_in), lambda b, s: (idx, 0, 0)),
            _const_spec((None, LANES, key), lambda b, s: (idx, 0, 0)),
            _const_spec((None, 1, key), lambda b, s: (idx, 0, 0)),
            _const_spec((None, SC_KERNEL, sc), lambda b, s: (idx, 0, 0)),
            _const_spec((None, 1, dv), lambda b, s: (idx, 0, 0)),
            _const_spec((None, mix, d), lambda b, s: (idx, 0, 0)),
            _const_spec((None, 1, d), lambda b, s: (layer, 0, 0)),
            _const_spec((None, 1, d), lambda b, s: (layer, 0, 0)),
            _const_spec((None, d, hidden), lambda b, s: (layer, 0, 0)),
            _const_spec((None, hidden, d), lambda b, s: (layer, 0, 0)),
        ],
        out_specs=o_spec,
        out_shape=jax.ShapeDtypeStruct((bsz, seq, d), F32),
        scratch_shapes=[
            pltpu.VMEM((rows, d), F32),
            pltpu.VMEM((rows, d), BF16),
            pltpu.VMEM((rows, n_in), F32),
            pltpu.VMEM((rows, mix), BF16),
            pltpu.VMEM((rows + SUBLANES, sc), F32),
            pltpu.VMEM((GLA_HEADS, dk, dv), F32),
            pltpu.VMEM((rows, key), F32),
            pltpu.VMEM((GLA_HEADS, dk, dv), F32),
            pltpu.VMEM((chunk, dv), F32),
        ],
        compiler_params=pltpu.CompilerParams(
            dimension_semantics=("arbitrary", "arbitrary"),
            vmem_limit_bytes=VMEM_LIMIT_BYTES),
        name="shortconv_gla_layer",
    )(x3d, g_pre, g_post, w_in, w_gk2, b_gk2, w_sc, gn_g, w_out, mg_pre, mg_post, w1, w2)


def _depthwise_conv(ubuf_ref, wdw_ref, bdw_ref, dbuf_ref, *, rows, d_model):
    sub = SUBLANES
    halo_blocks = CF_HALO // sub
    n_blocks = rows // sub
    max_a = (CF_KERNEL - 1) // sub
    assert max_a < halo_blocks
    row_id = lax.broadcasted_iota(jnp.int32, (sub, LANES), 0)
    for lg in range(d_model // LANES):
        lanes = slice(lg * LANES, (lg + 1) * LANES)
        w = {}
        for s in range(CF_KERNEL):
            w[s] = jnp.broadcast_to(wdw_ref[CF_KERNEL - 1 - s:CF_KERNEL - s, lanes], (sub, LANES))
        bias = jnp.broadcast_to(bdw_ref[:, lanes], (sub, LANES))
        blocks = {}

        def block(m):
            if m not in blocks:
                blocks[m] = ubuf_ref[m * sub:(m + 1) * sub, lanes]
            return blocks[m]

        prev_rolled = None
        for m in range(halo_blocks - 1, halo_blocks + n_blocks):
            q = []
            for r in range(sub):
                acc = None
                for a in range(max_a + 1):
                    s = sub * a + r
                    if s >= CF_KERNEL:
                        continue
                    term = w[s] * block(m - a)
                    acc = term if acc is None else acc + term
                q.append(acc)
            rolled = [None] + [pltpu.roll(q[r], r, axis=0) for r in range(1, sub)]
            if m >= halo_blocks:
                out = bias + q[0]
                for r in range(1, sub):
                    out = out + jnp.where(row_id >= r, rolled[r], prev_rolled[r])
                k = m - halo_blocks
                dbuf_ref[k * sub:(k + 1) * sub, lanes] = out
            prev_rolled = rolled
            blocks.pop(m - max_a, None)
        yield


def _odd_kernel(x_ref, gpre_ref, gpost_ref, wpw1_ref, bpw1_ref, wdw_ref, bdw_ref,
                lng_ref, lnb_ref, wpw2_ref, bpw2_ref, mgpre_ref, mgpost_ref, w1_ref, w2_ref,
                o_ref, x1buf_ref, ubuf_ref, dbuf_ref, *, rows, d_model):
    @pl.when(pl.program_id(1) == 0)
    def _():
        x1buf_ref[...] = jnp.zeros_like(x1buf_ref)
        ubuf_ref[0:CF_HALO, :] = jnp.zeros((CF_HALO, d_model), F32)

    h = _rms(x_ref[...], gpre_ref[...]).astype(BF16)
    u = jnp.dot(h, wpw1_ref[...], preferred_element_type=F32) + bpw1_ref[...]
    ubuf_ref[CF_HALO:CF_HALO + rows, :] = u[:, :d_model] * jax.nn.sigmoid(u[:, d_model:])

    mlp_done = []
    h_mlp = _rms(x1buf_ref[...], mgpre_ref[...]).astype(BF16)
    _interleave(
        _mlp_stage(x1buf_ref, h_mlp, mgpost_ref, w1_ref, w2_ref, o_ref, mlp_done, tail_parts=1),
        _depthwise_conv(ubuf_ref, wdw_ref, bdw_ref, dbuf_ref, rows=rows, d_model=d_model))
    ubuf_ref[0:CF_HALO, :] = ubuf_ref[rows:rows + CF_HALO, :]

    _order_after(dbuf_ref, (slice(0, SUBLANES), slice(0, LANES)), mlp_done[0])
    d = dbuf_ref[...]
    mu = jnp.mean(d, axis=-1, keepdims=True)
    dc = d - mu
    dn = dc * lax.rsqrt(jnp.mean(dc * dc, axis=-1, keepdims=True) + EPS)
    dn = dn * lng_ref[...] + lnb_ref[...]
    act = (dn * jax.nn.sigmoid(dn)).astype(BF16)
    y = jnp.dot(act, wpw2_ref[...], preferred_element_type=F32) + bpw2_ref[...]
    x1buf_ref[...] = x_ref[...] + _rms(y, gpost_ref[...])


def _odd_call(x3d, g_pre, g_post, w_pw1, b_pw1, w_dw, b_dw, ln_g, ln_b, w_pw2, b_pw2,
              mg_pre, mg_post, w1, w2, layer, idx):
    bsz, seq, d = x3d.shape
    rows = TILE_ROWS
    assert seq % rows == 0 and rows % SUBLANES == 0 and rows >= CF_HALO
    n_tiles = seq // rows
    hidden = w1.shape[-1]
    x_spec, o_spec = _tile_specs(rows, d, n_tiles)
    kern = functools.partial(_odd_kernel, rows=rows, d_model=d)
    return pl.pallas_call(
        kern,
        grid=(bsz, n_tiles + 1),
        in_specs=[
            x_spec,
            _const_spec((None, 1, d), lambda b, s: (layer, 0, 0)),
            _const_spec((None, 1, d), lambda b, s: (layer, 0, 0)),
            _const_spec((None, d, 2 * d), lambda b, s: (idx, 0, 0)),
            _const_spec((None, 1, 2 * d), lambda b, s: (idx, 0, 0)),
            _const_spec((None, CF_KERNEL, d), lambda b, s: (idx, 0, 0)),
            _const_spec((None, 1, d), lambda b, s: (idx, 0, 0)),
            _const_spec((None, 1, d), lambda b, s: (idx, 0, 0)),
            _const_spec((None, 1, d), lambda b, s: (idx, 0, 0)),
            _const_spec((None, d, d), lambda b, s: (idx, 0, 0)),
            _const_spec((None, 1, d), lambda b, s: (idx, 0, 0)),
            _const_spec((None, 1, d), lambda b, s: (layer, 0, 0)),
            _const_spec((None, 1, d), lambda b, s: (layer, 0, 0)),
            _const_spec((None, d, hidden), lambda b, s: (layer, 0, 0)),
            _const_spec((None, hidden, d), lambda b, s: (layer, 0, 0)),
        ],
        out_specs=o_spec,
        out_shape=jax.ShapeDtypeStruct((bsz, seq, d), F32),
        scratch_shapes=[
            pltpu.VMEM((rows, d), F32),
            pltpu.VMEM((rows + CF_HALO, d), F32),
            pltpu.VMEM((rows, d), F32),
        ],
        compiler_params=pltpu.CompilerParams(
            dimension_semantics=("arbitrary", "arbitrary"),
            vmem_limit_bytes=VMEM_LIMIT_BYTES),
        name="conformer_layer",
    )(x3d, g_pre, g_post, w_pw1, b_pw1, w_dw, b_dw, ln_g, ln_b, w_pw2, b_pw2,
      mg_pre, mg_post, w1, w2)


def kernel(x, norm_mix_pre, norm_mix_post, norm_mlp_pre, norm_mlp_post,
           ab_w_in, gla_w_gk2, gla_b_gk2, sc_w_conv, gla_norm_g, ab_w_out,
           cf_w_pw1, cf_b_pw1, cf_w_dw, cf_b_dw, cf_ln_g, cf_ln_b, cf_w_pw2, cf_b_pw2,
           mlp_w1, mlp_w2):
    depth = norm_mix_pre.shape[0]
    n_in = ab_w_in.shape[-1]
    n_main = n_in - GLA_RANK
    assert n_main % LANES == 0 and mlp_w1.shape[-1] % MLP_HIDDEN_CHUNK == 0

    n_pad = -(-(n_main + LANES) // MXU_DIM) * MXU_DIM
    w_in = jnp.pad(ab_w_in.astype(BF16), ((0, 0), (0, 0), (0, n_pad - n_in)))
    w_gk2 = jnp.pad(gla_w_gk2.astype(BF16), ((0, 0), (0, LANES - GLA_RANK), (0, 0)))
    w_out = ab_w_out.astype(BF16)
    w_pw1 = cf_w_pw1.astype(BF16)
    w_pw2 = cf_w_pw2.astype(BF16)
    w1 = mlp_w1.astype(BF16)
    w2 = mlp_w2.astype(BF16)

    def rows3d(v):
        return v.reshape(v.shape[0], 1, v.shape[1])

    norm_mix_pre, norm_mix_post, norm_mlp_pre, norm_mlp_post = map(
        rows3d, (norm_mix_pre, norm_mix_post, norm_mlp_pre, norm_mlp_post))
    gla_b_gk2, gla_norm_g = rows3d(gla_b_gk2), rows3d(gla_norm_g)
    cf_b_pw1, cf_b_dw, cf_ln_g, cf_ln_b, cf_b_pw2 = map(
        rows3d, (cf_b_pw1, cf_b_dw, cf_ln_g, cf_ln_b, cf_b_pw2))

    for layer in range(depth):
        idx = layer // 2
        if layer % 2 == 0:
            x = _even_call(x, norm_mix_pre, norm_mix_post, w_in, w_gk2, gla_b_gk2,
                           sc_w_conv, gla_norm_g, w_out, norm_mlp_pre, norm_mlp_post,
                           w1, w2, layer, idx)
        else:
            x = _odd_call(x, norm_mix_pre, norm_mix_post, w_pw1, cf_b_pw1, cf_w_dw,
                          cf_b_dw, cf_ln_g, cf_ln_b, w_pw2, cf_b_pw2, norm_mlp_pre,
                          norm_mlp_post, w1, w2, layer, idx)
    return x
```

```python
import functools

import jax
import jax.numpy as jnp
from jax import lax
from jax.experimental import pallas as pl
from jax.experimental.pallas import tpu as pltpu

F32 = jnp.float32
BF16 = jnp.bfloat16

EPS = 1e-6
GLA_HEADS = 4
GLA_RANK = 16
GLA_GATE_NORM = 16.0
SC_KERNEL = 3
CF_KERNEL = 31

LANES = 128
SUBLANES = 8
MXU_DIM = 256
VMEM_LIMIT_BYTES = 56 * 1024 * 1024

TILE_ROWS = 256
MLP_HIDDEN_CHUNK = 1024
GLA_CHUNK = 128
EVEN_HEAD_ORDER = (0, 0, 0)
GLA_FAST_PATH_MAX_DECAY = 60.0
CF_HALO = 32
def _rms(x, g):
    ms = jnp.mean(x * x, axis=-1, keepdims=True)
    return x * lax.rsqrt(ms + EPS) * g


def _const_spec(shape, index_map):
    return pl.BlockSpec(shape, index_map, pipeline_mode=pl.Buffered(1))


def _zero_bits_of(v):
    u = lax.bitcast_convert_type(v, jnp.uint32)
    return lax.shift_right_logical(lax.shift_right_logical(u, jnp.uint32(16)), jnp.uint32(16))


def _order_after(ref, block, token):
    bits = lax.bitcast_convert_type(ref[block], jnp.uint32) | token
    ref[block] = lax.bitcast_convert_type(bits, F32)


def _store_mixer_output(x1, mgpre_ref, x1buf_ref, hbuf_ref):
    x1buf_ref[...] = x1
    hbuf_ref[...] = _rms(x1, mgpre_ref[...]).astype(BF16)


def _mlp_stage(x1buf_ref, h, gpost_ref, w1_ref, w2_ref, o_ref, done, *, tail_parts):
    rows = h.shape[0]
    part = rows // tail_parts
    hidden = w1_ref.shape[1]
    n_chunks = hidden // MLP_HIDDEN_CHUNK

    def cols(c):
        return slice(c * MLP_HIDDEN_CHUNK, (c + 1) * MLP_HIDDEN_CHUNK)

    def up(c):
        raw = jnp.dot(h, w1_ref[:, cols(c)], preferred_element_type=F32)
        raw = jnp.maximum(raw, 0.0)
        return (raw * raw).astype(BF16)

    acc = None
    act = up(0)
    yield
    for c in range(1, n_chunks):
        act_next = up(c)
        yield
        p = jnp.dot(act, w2_ref[cols(c - 1), :], preferred_element_type=F32)
        acc = p if acc is None else acc + p
        act = act_next
        yield
    out = None
    for t in range(tail_parts):
        sl = slice(t * part, (t + 1) * part)
        p = jnp.dot(act[sl], w2_ref[cols(n_chunks - 1), :], preferred_element_type=F32)
        total = p if acc is None else acc[sl] + p
        out = x1buf_ref[sl, :] + _rms(total, gpost_ref[...])
        o_ref[sl, :] = out
    done.append(_zero_bits_of(out[0:SUBLANES, 0:LANES]))


def _interleave(*gens, head=()):
    live = list(gens)

    def step(g):
        try:
            next(g)
        except StopIteration:
            live.remove(g)

    for i in head:
        step(gens[i])
    while live:
        for g in list(live):
            step(g)


def _even_mixer_stage(x_ref, gpre_ref, win_ref, wgk2_ref, bgk2_ref, wsc_ref, gng_ref, wout_ref,
                      z_ref, mix_ref, ubuf_ref, s_ref, bbuf_ref, sprev_ref, result,
                      *, rows, chunk, d_model):
    sc = d_model // 2
    dk = d_model // 2 // GLA_HEADS
    dv = d_model // GLA_HEADS
    key = GLA_HEADS * dk
    o_scx, o_scb, o_scc = 0, sc, 2 * sc
    o_q = 3 * sc
    o_k = o_q + key
    o_v = o_k + key
    o_g = o_v + GLA_HEADS * dv
    o_lr = o_g + GLA_HEADS * dv

    h = _rms(x_ref[...], gpre_ref[...]).astype(BF16)
    z_ref[...] = jnp.dot(h, win_ref[...], preferred_element_type=F32)
    yield

    gk_lr = z_ref[:, o_lr:o_lr + LANES].astype(BF16)
    gk = jnp.dot(gk_lr, wgk2_ref[...], preferred_element_type=F32) + bgk2_ref[...]
    yield
    g = (jnp.minimum(gk, 0.0) - jnp.log1p(jnp.exp(-jnp.abs(gk)))) * (1.0 / GLA_GATE_NORM)
    g_hi = g.astype(BF16)
    g_lo = (g - g_hi.astype(F32)).astype(BF16)
    ri = lax.broadcasted_iota(jnp.int32, (rows, rows), 0)
    ci = lax.broadcasted_iota(jnp.int32, (rows, rows), 1)
    same_chunk = (ri // chunk) == (ci // chunk)
    tril_chunks = jnp.where(same_chunk & (ci <= ri), 1.0, 0.0).astype(BF16)
    b = (jnp.dot(tril_chunks, g_hi, preferred_element_type=F32)
         + jnp.dot(tril_chunks, g_lo, preferred_element_type=F32))
    bbuf_ref[...] = b
    sprev_ref[...] = s_ref[...]
    n_chunks = rows // chunk
    chunk_totals = jnp.concatenate(
        [b[(c + 1) * chunk - 1:(c + 1) * chunk, :] for c in range(n_chunks)], axis=0)
    result.append(jnp.min(chunk_totals))
    yield

    u = z_ref[:, o_scc:o_scc + sc] * z_ref[:, o_scx:o_scx + sc]
    ubuf_ref[SUBLANES:SUBLANES + rows, :] = u
    conv = wsc_ref[SC_KERNEL - 1:SC_KERNEL, :] * u
    for j in range(SC_KERNEL - 1):
        shift = SC_KERNEL - 1 - j
        conv = conv + wsc_ref[j:j + 1, :] * ubuf_ref[SUBLANES - shift:SUBLANES - shift + rows, :]
    ubuf_ref[0:SUBLANES, :] = u[rows - SUBLANES:rows, :]
    mix_ref[:, 0:sc] = (z_ref[:, o_scb:o_scb + sc] * conv).astype(BF16)

    causal = (lax.broadcasted_iota(jnp.int32, (chunk, chunk), 1)
              <= lax.broadcasted_iota(jnp.int32, (chunk, chunk), 0))
    scale = dk ** -0.5
    nt_dims = (((1,), (1,)), ((), ()))
    tn_dims = (((0,), (0,)), ((), ()))
    gng = gng_ref[...]
    pairs = [(hd, c) for hd in range(GLA_HEADS) for c in range(n_chunks)]

    q_in, v_bf, decay, attn, kv = {}, {}, {}, {}, {}
    for hd, c in pairs:
        r0 = c * chunk
        bc = b[r0:r0 + chunk, hd * dk:(hd + 1) * dk]
        qc = z_ref[r0:r0 + chunk, o_q + hd * dk:o_q + (hd + 1) * dk]
        kc = z_ref[r0:r0 + chunk, o_k + hd * dk:o_k + (hd + 1) * dk]
        v_bf[hd, c] = z_ref[r0:r0 + chunk, o_v + hd * dv:o_v + (hd + 1) * dv].astype(BF16)
        b_last = bc[chunk - 1:chunk, :]
        b_mid = bc[chunk // 2 - 1:chunk // 2, :]
        qd = qc * jnp.exp(bc) * scale
        q_mid = (qd * jnp.exp(-b_mid)).astype(BF16)
        k_mid = kc * jnp.exp(b_mid - bc)
        k_end = (kc * jnp.exp(b_last - bc)).astype(BF16)
        q_in[hd, c] = qd.astype(BF16)
        decay[hd, c] = jnp.broadcast_to(jnp.exp(b_last), (dk, dk)).T
        attn[hd, c] = lax.dot_general(q_mid, k_mid.astype(BF16), nt_dims,
                                      preferred_element_type=F32)
        kv[hd, c] = lax.dot_general(k_end, v_bf[hd, c], tn_dims, preferred_element_type=F32)
    yield

    outs = {}
    for hd in range(GLA_HEADS):
        state = s_ref[hd]
        for c in range(n_chunks):
            a = jnp.where(causal, attn[hd, c], 0.0).astype(BF16)
            outs[hd, c] = (jnp.dot(q_in[hd, c], state.astype(BF16), preferred_element_type=F32)
                           + jnp.dot(a, v_bf[hd, c], preferred_element_type=F32))
            state = (state * jnp.concatenate([decay[hd, c]] * (dv // dk), axis=1) + kv[hd, c])
        s_ref[hd] = state
    yield

    for hd, c in pairs:
        r0 = c * chunk
        o = outs[hd, c]
        gout = z_ref[r0:r0 + chunk, o_g + hd * dv:o_g + (hd + 1) * dv]
        on = o * lax.rsqrt(jnp.mean(o * o, axis=-1, keepdims=True) + EPS) * gng
        yb = on * (gout * jax.nn.sigmoid(gout))
        mix_ref[r0:r0 + chunk, sc + hd * dv:sc + (hd + 1) * dv] = yb.astype(BF16)
    result.append(jnp.dot(mix_ref[...], wout_ref[...], preferred_element_type=F32))


def _even_exact_tile(x_ref, gpost_ref, gng_ref, wout_ref, mgpre_ref, x1buf_ref, hbuf_ref,
                     z_ref, mix_ref, bbuf_ref, sprev_ref, orow_ref, *, rows, chunk, d_model):
    sc = d_model // 2
    dk = d_model // 2 // GLA_HEADS
    dv = d_model // GLA_HEADS
    key = GLA_HEADS * dk
    o_q = 3 * sc
    o_k = o_q + key
    o_v = o_k + key
    o_g = o_v + GLA_HEADS * dv
    scale = dk ** -0.5
    tn_dims = (((0,), (0,)), ((), ()))
    gng = gng_ref[...]
    row_id = lax.broadcasted_iota(jnp.int32, (chunk, 1), 0)
    for hd in range(GLA_HEADS):
        klanes = slice(hd * dk, (hd + 1) * dk)
        qlanes = slice(o_q + hd * dk, o_q + (hd + 1) * dk)
        state = sprev_ref[hd]
        for c in range(rows // chunk):
            r0 = c * chunk
            bc = bbuf_ref[r0:r0 + chunk, klanes]
            kc = z_ref[r0:r0 + chunk, o_k + hd * dk:o_k + (hd + 1) * dk]
            vc = z_ref[r0:r0 + chunk, o_v + hd * dv:o_v + (hd + 1) * dv]
            gout = z_ref[r0:r0 + chunk, o_g + hd * dv:o_g + (hd + 1) * dv]
            b_last = bc[chunk - 1:chunk, :]
            q_in = (z_ref[r0:r0 + chunk, qlanes] * jnp.exp(bc) * scale).astype(BF16)
            k_end = (kc * jnp.exp(b_last - bc)).astype(BF16)

            def eight_rows(blk, carry):
                base = pl.multiple_of(blk * SUBLANES, SUBLANES)
                b_blk = bbuf_ref[pl.ds(r0 + base, SUBLANES), klanes]
                q_blk = z_ref[pl.ds(r0 + base, SUBLANES), qlanes] * scale
                out_rows = []
                for r in range(SUBLANES):
                    decay_ij = jnp.exp(jnp.minimum(b_blk[r:r + 1, :] - bc, 0.0))
                    w = (q_blk[r:r + 1, :] * decay_ij) * kc
                    a = jnp.where(row_id <= base + r, jnp.sum(w, axis=-1, keepdims=True), 0.0)
                    out_rows.append(jnp.sum(a * vc, axis=0, keepdims=True))
                orow_ref[pl.ds(base, SUBLANES), :] = jnp.concatenate(out_rows, axis=0)
                return carry

            lax.fori_loop(0, chunk // SUBLANES, eight_rows, 0)
            o = (jnp.dot(q_in, state.astype(BF16), preferred_element_type=F32) + orow_ref[...])
            decay = jnp.broadcast_to(jnp.exp(b_last), (dk, dk)).T
            state = (state * jnp.concatenate([decay] * (dv // dk), axis=1)
                     + lax.dot_general(k_end, vc.astype(BF16), tn_dims,
                                       preferred_element_type=F32))
            on = o * lax.rsqrt(jnp.mean(o * o, axis=-1, keepdims=True) + EPS) * gng
            yb = on * (gout * jax.nn.sigmoid(gout))
            mix_ref[r0:r0 + chunk, sc + hd * dv:sc + (hd + 1) * dv] = yb.astype(BF16)
    y = jnp.dot(mix_ref[...], wout_ref[...], preferred_element_type=F32)
    _store_mixer_output(x_ref[...] + _rms(y, gpost_ref[...]), mgpre_ref, x1buf_ref, hbuf_ref)


def _even_kernel(x_ref, gpre_ref, gpost_ref, win_ref, wgk2_ref, bgk2_ref, wsc_ref, gng_ref,
                 wout_ref, mgpre_ref, mgpost_ref, w1_ref, w2_ref, o_ref,
                 x1buf_ref, hbuf_ref, z_ref, mix_ref, ubuf_ref, s_ref, bbuf_ref, sprev_ref,
                 orow_ref, *, rows, chunk, d_model):
    @pl.when(pl.program_id(1) == 0)
    def _():
        x1buf_ref[...] = jnp.zeros_like(x1buf_ref)
        hbuf_ref[...] = jnp.zeros_like(hbuf_ref)
        s_ref[...] = jnp.zeros_like(s_ref)
        ubuf_ref[0:SUBLANES, :] = jnp.zeros((SUBLANES, ubuf_ref.shape[1]), F32)

    mixed, mlp_done = [], []
    _interleave(
        _mlp_stage(x1buf_ref, hbuf_ref[...], mgpost_ref, w1_ref, w2_ref, o_ref, mlp_done,
                   tail_parts=2),
        _even_mixer_stage(x_ref, gpre_ref, win_ref, wgk2_ref, bgk2_ref, wsc_ref, gng_ref,
                          wout_ref, z_ref, mix_ref, ubuf_ref, s_ref, bbuf_ref, sprev_ref, mixed,
                          rows=rows, chunk=chunk, d_model=d_model),
        head=EVEN_HEAD_ORDER)
    min_chunk_decay, y = mixed
    _store_mixer_output(x_ref[...] + _rms(y, gpost_ref[...]), mgpre_ref, x1buf_ref, hbuf_ref)

    @pl.when(min_chunk_decay < -GLA_FAST_PATH_MAX_DECAY)
    def _():
        _even_exact_tile(x_ref, gpost_ref, gng_ref, wout_ref, mgpre_ref, x1buf_ref, hbuf_ref,
                         z_ref, mix_ref, bbuf_ref, sprev_ref, orow_ref,
                         rows=rows, chunk=chunk, d_model=d_model)


def _tile_specs(rows, d, n_tiles):
    x_spec = pl.BlockSpec((None, rows, d), lambda b, s: (b, jnp.minimum(s, n_tiles - 1), 0))
    o_spec = pl.BlockSpec((None, rows, d), lambda b, s: (b, jnp.maximum(s - 1, 0), 0))
    return x_spec, o_spec


def _even_call(x3d, g_pre, g_post, w_in, w_gk2, b_gk2, w_sc, gn_g, w_out,
               mg_pre, mg_post, w1, w2, layer, idx):
    bsz, seq, d = x3d.shape
    rows, chunk = TILE_ROWS, GLA_CHUNK
    assert seq % rows == 0 and rows % chunk == 0
    n_tiles = seq // rows
    n_in = w_in.shape[-1]
    key = w_gk2.shape[-1]
    sc = w_sc.shape[-1]
    mix = w_out.shape[1]
    hidden = w1.shape[-1]
    dk = key // GLA_HEADS
    dv = d // GLA_HEADS
    x_spec, o_spec = _tile_specs(rows, d, n_tiles)
    kern = functools.partial(_even_kernel, rows=rows, chunk=chunk, d_model=d)
    return pl.pallas_call(
        kern,
        grid=(bsz, n_tiles + 1),
        in_specs=[
            x_spec,
            _const_spec((None, 1, d), lambda b, s: (layer, 0, 0)),
            _const_spec((None, 1, d), lambda b, s: (layer, 0, 0)),
            _const_spec((None, d, n_in), lambda b, s: (idx, 0, 0)),
            _const_spec((None, LANES, key), lambda b, s: (idx, 0, 0)),
            _const_spec((None, 1, key), lambda b, s: (idx, 0, 0)),
            _const_spec((None, SC_KERNEL, sc), lambda b, s: (idx, 0, 0)),
            _const_spec((None, 1, dv), lambda b, s: (idx, 0, 0)),
            _const_spec((None, mix, d), lambda b, s: (idx, 0, 0)),
            _const_spec((None, 1, d), lambda b, s: (layer, 0, 0)),
            _const_spec((None, 1, d), lambda b, s: (layer, 0, 0)),
            _const_spec((None, d, hidden), lambda b, s: (layer, 0, 0)),
            _const_spec((None, hidden, d), lambda b, s: (layer, 0, 0)),
        ],
        out_specs=o_spec,
        out_shape=jax.ShapeDtypeStruct((bsz, seq, d), F32),
        scratch_shapes=[
            pltpu.VMEM((rows, d), F32),
            pltpu.VMEM((rows, d), BF16),
            pltpu.VMEM((rows, n_in), F32),
            pltpu.VMEM((rows, mix), BF16),
            pltpu.VMEM((rows + SUBLANES, sc), F32),
            pltpu.VMEM((GLA_HEADS, dk, dv), F32),
            pltpu.VMEM((rows, key), F32),
            pltpu.VMEM((GLA_HEADS, dk, dv), F32),
            pltpu.VMEM((chunk, dv), F32),
        ],
        compiler_params=pltpu.CompilerParams(
            dimension_semantics=("arbitrary", "arbitrary"),
            vmem_limit_bytes=VMEM_LIMIT_BYTES),
        name="shortconv_gla_layer",
    )(x3d, g_pre, g_post, w_in, w_gk2, b_gk2, w_sc, gn_g, w_out, mg_pre, mg_post, w1, w2)


def _depthwise_conv(ubuf_ref, wdw_ref, bdw_ref, dbuf_ref, *, rows, d_model):
    sub = SUBLANES
    halo_blocks = CF_HALO // sub
    n_blocks = rows // sub
    max_a = (CF_KERNEL - 1) // sub
    assert max_a < halo_blocks
    row_id = lax.broadcasted_iota(jnp.int32, (sub, LANES), 0)
    for lg in range(d_model // LANES):
        lanes = slice(lg * LANES, (lg + 1) * LANES)
        w = {}
        for s in range(CF_KERNEL):
            w[s] = jnp.broadcast_to(wdw_ref[CF_KERNEL - 1 - s:CF_KERNEL - s, lanes], (sub, LANES))
        bias = jnp.broadcast_to(bdw_ref[:, lanes], (sub, LANES))
        blocks = {}

        def block(m):
            if m not in blocks:
                blocks[m] = ubuf_ref[m * sub:(m + 1) * sub, lanes]
            return blocks[m]

        prev_q = None
        for m in range(halo_blocks - 1, halo_blocks + n_blocks):
            q = []
            for r in range(sub):
                acc = None
                for a in range(max_a + 1):
                    s = sub * a + r
                    if s >= CF_KERNEL:
                        continue
                    term = w[s] * block(m - a)
                    acc = term if acc is None else acc + term
                q.append(acc)
            if m >= halo_blocks:
                out = bias + q[0]
                for r in range(1, sub):
                    both = jnp.where(row_id < sub - r, q[r], prev_q[r])
                    out = out + pltpu.roll(both, r, axis=0)
                k = m - halo_blocks
                dbuf_ref[k * sub:(k + 1) * sub, lanes] = out
            prev_q = q
            blocks.pop(m - max_a, None)
        yield


def _odd_kernel(x_ref, gpre_ref, gpost_ref, wpw1_ref, bpw1_ref, wdw_ref, bdw_ref,
                lng_ref, lnb_ref, wpw2_ref, bpw2_ref, mgpre_ref, mgpost_ref, w1_ref, w2_ref,
                o_ref, x1buf_ref, ubuf_ref, dbuf_ref, *, rows, d_model):
    @pl.when(pl.program_id(1) == 0)
    def _():
        x1buf_ref[...] = jnp.zeros_like(x1buf_ref)
        ubuf_ref[0:CF_HALO, :] = jnp.zeros((CF_HALO, d_model), F32)

    h = _rms(x_ref[...], gpre_ref[...]).astype(BF16)
    u = jnp.dot(h, wpw1_ref[...], preferred_element_type=F32) + bpw1_ref[...]
    ubuf_ref[CF_HALO:CF_HALO + rows, :] = u[:, :d_model] * jax.nn.sigmoid(u[:, d_model:])

    mlp_done = []
    h_mlp = _rms(x1buf_ref[...], mgpre_ref[...]).astype(BF16)
    _interleave(
        _mlp_stage(x1buf_ref, h_mlp, mgpost_ref, w1_ref, w2_ref, o_ref, mlp_done, tail_parts=1),
        _depthwise_conv(ubuf_ref, wdw_ref, bdw_ref, dbuf_ref, rows=rows, d_model=d_model))
    ubuf_ref[0:CF_HALO, :] = ubuf_ref[rows:rows + CF_HALO, :]

    _order_after(dbuf_ref, (slice(0, SUBLANES), slice(0, LANES)), mlp_done[-1])
    d = dbuf_ref[...]
    mu = jnp.mean(d, axis=-1, keepdims=True)
    dc = d - mu
    dn = dc * lax.rsqrt(jnp.mean(dc * dc, axis=-1, keepdims=True) + EPS)
    dn = dn * lng_ref[...] + lnb_ref[...]
    act = (dn * jax.nn.sigmoid(dn)).astype(BF16)
    y = jnp.dot(act, wpw2_ref[...], preferred_element_type=F32) + bpw2_ref[...]
    x1buf_ref[...] = x_ref[...] + _rms(y, gpost_ref[...])


def _odd_call(x3d, g_pre, g_post, w_pw1, b_pw1, w_dw, b_dw, ln_g, ln_b, w_pw2, b_pw2,
              mg_pre, mg_post, w1, w2, layer, idx):
    bsz, seq, d = x3d.shape
    rows = TILE_ROWS
    assert seq % rows == 0 and rows % SUBLANES == 0 and rows >= CF_HALO
    n_tiles = seq // rows
    hidden = w1.shape[-1]
    x_spec, o_spec = _tile_specs(rows, d, n_tiles)
    kern = functools.partial(_odd_kernel, rows=rows, d_model=d)
    return pl.pallas_call(
        kern,
        grid=(bsz, n_tiles + 1),
        in_specs=[
            x_spec,
            _const_spec((None, 1, d), lambda b, s: (layer, 0, 0)),
            _const_spec((None, 1, d), lambda b, s: (layer, 0, 0)),
            _const_spec((None, d, 2 * d), lambda b, s: (idx, 0, 0)),
            _const_spec((None, 1, 2 * d), lambda b, s: (idx, 0, 0)),
            _const_spec((None, CF_KERNEL, d), lambda b, s: (idx, 0, 0)),
            _const_spec((None, 1, d), lambda b, s: (idx, 0, 0)),
            _const_spec((None, 1, d), lambda b, s: (idx, 0, 0)),
            _const_spec((None, 1, d), lambda b, s: (idx, 0, 0)),
            _const_spec((None, d, d), lambda b, s: (idx, 0, 0)),
            _const_spec((None, 1, d), lambda b, s: (idx, 0, 0)),
            _const_spec((None, 1, d), lambda b, s: (layer, 0, 0)),
            _const_spec((None, 1, d), lambda b, s: (layer, 0, 0)),
            _const_spec((None, d, hidden), lambda b, s: (layer, 0, 0)),
            _const_spec((None, hidden, d), lambda b, s: (layer, 0, 0)),
        ],
        out_specs=o_spec,
        out_shape=jax.ShapeDtypeStruct((bsz, seq, d), F32),
        scratch_shapes=[
            pltpu.VMEM((rows, d), F32),
            pltpu.VMEM((rows + CF_HALO, d), F32),
            pltpu.VMEM((rows, d), F32),
        ],
        compiler_params=pltpu.CompilerParams(
            dimension_semantics=("arbitrary", "arbitrary"),
            vmem_limit_bytes=VMEM_LIMIT_BYTES),
        name="conformer_layer",
    )(x3d, g_pre, g_post, w_pw1, b_pw1, w_dw, b_dw, ln_g, ln_b, w_pw2, b_pw2,
      mg_pre, mg_post, w1, w2)


def kernel(x, norm_mix_pre, norm_mix_post, norm_mlp_pre, norm_mlp_post,
           ab_w_in, gla_w_gk2, gla_b_gk2, sc_w_conv, gla_norm_g, ab_w_out,
           cf_w_pw1, cf_b_pw1, cf_w_dw, cf_b_dw, cf_ln_g, cf_ln_b, cf_w_pw2, cf_b_pw2,
           mlp_w1, mlp_w2):
    depth = norm_mix_pre.shape[0]
    n_in = ab_w_in.shape[-1]
    n_main = n_in - GLA_RANK
    assert n_main % LANES == 0 and mlp_w1.shape[-1] % MLP_HIDDEN_CHUNK == 0

    n_pad = -(-(n_main + LANES) // MXU_DIM) * MXU_DIM
    w_in = jnp.pad(ab_w_in.astype(BF16), ((0, 0), (0, 0), (0, n_pad - n_in)))
    w_gk2 = jnp.pad(gla_w_gk2.astype(BF16), ((0, 0), (0, LANES - GLA_RANK), (0, 0)))
    w_out = ab_w_out.astype(BF16)
    w_pw1 = cf_w_pw1.astype(BF16)
    w_pw2 = cf_w_pw2.astype(BF16)
    w1 = mlp_w1.astype(BF16)
    w2 = mlp_w2.astype(BF16)

    def rows3d(v):
        return v.reshape(v.shape[0], 1, v.shape[1])

    norm_mix_pre, norm_mix_post, norm_mlp_pre, norm_mlp_post = map(
        rows3d, (norm_mix_pre, norm_mix_post, norm_mlp_pre, norm_mlp_post))
    gla_b_gk2, gla_norm_g = rows3d(gla_b_gk2), rows3d(gla_norm_g)
    cf_b_pw1, cf_b_dw, cf_ln_g, cf_ln_b, cf_b_pw2 = map(
        rows3d, (cf_b_pw1, cf_b_dw, cf_ln_g, cf_ln_b, cf_b_pw2))

    for layer in range(depth):
        idx = layer // 2
        if layer % 2 == 0:
            x = _even_call(x, norm_mix_pre, norm_mix_post, w_in, w_gk2, gla_b_gk2,
                           sc_w_conv, gla_norm_g, w_out, norm_mlp_pre, norm_mlp_post,
                           w1, w2, layer, idx)
        else:
            x = _odd_call(x, norm_mix_pre, norm_mix_post, w_pw1, cf_b_pw1, cf_w_dw,
                          cf_b_dw, cf_ln_g, cf_ln_b, w_pw2, cf_b_pw2, norm_mlp_pre,
                          norm_mlp_post, w1, w2, layer, idx)
    return x
```

```python
import functools

import jax
import jax.numpy as jnp
from jax import lax
from jax.experimental import pallas as pl
from jax.experimental.pallas import tpu as pltpu

F32 = jnp.float32
BF16 = jnp.bfloat16

EPS = 1e-6
GLA_HEADS = 4
GLA_RANK = 16
GLA_GATE_NORM = 16.0
SC_KERNEL = 3
CF_KERNEL = 31

LANES = 128
SUBLANES = 8
MXU_DIM = 256
VMEM_LIMIT_BYTES = 56 * 1024 * 1024
WEIGHT_STAGE_BYTES = 1024 * 1024

TILE_ROWS = 256
MLP_HIDDEN_CHUNK = 1024
GLA_CHUNK = 128
EVEN_HEAD_ORDER = (0, 0, 0)
GLA_FAST_PATH_MAX_DECAY = 60.0
CF_HALO = 32
def _rms(x, g):
    ms = jnp.mean(x * x, axis=-1, keepdims=True)
    return x * lax.rsqrt(ms + EPS) * g


def _const_spec(shape, index_map):
    return pl.BlockSpec(shape, index_map, pipeline_mode=pl.Buffered(1))


def _zero_bits_of(v):
    u = lax.bitcast_convert_type(v, jnp.uint32)
    return lax.shift_right_logical(lax.shift_right_logical(u, jnp.uint32(16)), jnp.uint32(16))


def _order_after(ref, block, token):
    bits = lax.bitcast_convert_type(ref[block], jnp.uint32) | token
    ref[block] = lax.bitcast_convert_type(bits, F32)


def _load_weight_as_bf16(src_hbm_ref, dst_ref, stage_ref, sem_ref):
    k_rows, n_cols = dst_ref.shape
    chunk = stage_ref.shape[1]
    assert k_rows % chunk == 0 and stage_ref.shape[2] == n_cols
    n_chunks = k_rows // chunk

    def copy(i):
        return pltpu.make_async_copy(src_hbm_ref.at[pl.ds(i * chunk, chunk), :],
                                     stage_ref.at[i % 2], sem_ref.at[i % 2])

    copy(0).start()
    for i in range(n_chunks):
        if i + 1 < n_chunks:
            copy(i + 1).start()
        copy(i).wait()
        dst_ref[i * chunk:(i + 1) * chunk, :] = stage_ref[i % 2].astype(BF16)


def _stage_rows(n_cols):
    return WEIGHT_STAGE_BYTES // (4 * n_cols)


def _store_mixer_output(x1, mgpre_ref, x1buf_ref, hbuf_ref):
    x1buf_ref[...] = x1
    hbuf_ref[...] = _rms(x1, mgpre_ref[...]).astype(BF16)


def _mlp_stage(x1buf_ref, h, gpost_ref, w1_ref, w2_ref, o_ref, done, *, tail_parts):
    rows = h.shape[0]
    part = rows // tail_parts
    hidden = w1_ref.shape[1]
    n_chunks = hidden // MLP_HIDDEN_CHUNK

    def cols(c):
        return slice(c * MLP_HIDDEN_CHUNK, (c + 1) * MLP_HIDDEN_CHUNK)

    def up(c):
        raw = jnp.dot(h, w1_ref[:, cols(c)], preferred_element_type=F32)
        raw = jnp.maximum(raw, 0.0)
        return (raw * raw).astype(BF16)

    acc = None
    act = up(0)
    yield
    for c in range(1, n_chunks):
        act_next = up(c)
        yield
        p = jnp.dot(act, w2_ref[cols(c - 1), :], preferred_element_type=F32)
        acc = p if acc is None else acc + p
        act = act_next
        yield
    out = None
    for t in range(tail_parts):
        sl = slice(t * part, (t + 1) * part)
        p = jnp.dot(act[sl], w2_ref[cols(n_chunks - 1), :], preferred_element_type=F32)
        total = p if acc is None else acc[sl] + p
        out = x1buf_ref[sl, :] + _rms(total, gpost_ref[...])
        o_ref[sl, :] = out
    done.append(_zero_bits_of(out[0:SUBLANES, 0:LANES]))


def _interleave(*gens, head=()):
    live = list(gens)

    def step(g):
        try:
            next(g)
        except StopIteration:
            live.remove(g)

    for i in head:
        step(gens[i])
    while live:
        for g in list(live):
            step(g)


def _even_mixer_stage(x_ref, gpre_ref, win_ref, wgk2_ref, bgk2_ref, wsc_ref, gng_ref, wout_ref,
                      z_ref, mix_ref, ubuf_ref, s_ref, bbuf_ref, sprev_ref, result,
                      *, rows, chunk, d_model):
    sc = d_model // 2
    dk = d_model // 2 // GLA_HEADS
    dv = d_model // GLA_HEADS
    key = GLA_HEADS * dk
    o_scx, o_scb, o_scc = 0, sc, 2 * sc
    o_q = 3 * sc
    o_k = o_q + key
    o_v = o_k + key
    o_g = o_v + GLA_HEADS * dv
    o_lr = o_g + GLA_HEADS * dv

    h = _rms(x_ref[...], gpre_ref[...]).astype(BF16)
    z_ref[...] = jnp.dot(h, win_ref[...], preferred_element_type=F32)
    yield

    gk_lr = z_ref[:, o_lr:o_lr + LANES].astype(BF16)
    gk = jnp.dot(gk_lr, wgk2_ref[...], preferred_element_type=F32) + bgk2_ref[...]
    yield
    g = (jnp.minimum(gk, 0.0) - jnp.log1p(jnp.exp(-jnp.abs(gk)))) * (1.0 / GLA_GATE_NORM)
    g_hi = g.astype(BF16)
    g_lo = (g - g_hi.astype(F32)).astype(BF16)
    ri = lax.broadcasted_iota(jnp.int32, (rows, rows), 0)
    ci = lax.broadcasted_iota(jnp.int32, (rows, rows), 1)
    same_chunk = (ri // chunk) == (ci // chunk)
    tril_chunks = jnp.where(same_chunk & (ci <= ri), 1.0, 0.0).astype(BF16)
    b = (jnp.dot(tril_chunks, g_hi, preferred_element_type=F32)
         + jnp.dot(tril_chunks, g_lo, preferred_element_type=F32))
    bbuf_ref[...] = b
    sprev_ref[...] = s_ref[...]
    n_chunks = rows // chunk
    chunk_totals = jnp.concatenate(
        [b[(c + 1) * chunk - 1:(c + 1) * chunk, :] for c in range(n_chunks)], axis=0)
    result.append(jnp.min(chunk_totals))
    yield

    u = z_ref[:, o_scc:o_scc + sc] * z_ref[:, o_scx:o_scx + sc]
    ubuf_ref[SUBLANES:SUBLANES + rows, :] = u
    conv = wsc_ref[SC_KERNEL - 1:SC_KERNEL, :] * u
    for j in range(SC_KERNEL - 1):
        shift = SC_KERNEL - 1 - j
        conv = conv + wsc_ref[j:j + 1, :] * ubuf_ref[SUBLANES - shift:SUBLANES - shift + rows, :]
    ubuf_ref[0:SUBLANES, :] = u[rows - SUBLANES:rows, :]
    mix_ref[:, 0:sc] = (z_ref[:, o_scb:o_scb + sc] * conv).astype(BF16)

    causal = (lax.broadcasted_iota(jnp.int32, (chunk, chunk), 1)
              <= lax.broadcasted_iota(jnp.int32, (chunk, chunk), 0))
    scale = dk ** -0.5
    nt_dims = (((1,), (1,)), ((), ()))
    tn_dims = (((0,), (0,)), ((), ()))
    gng = gng_ref[...]
    pairs = [(hd, c) for hd in range(GLA_HEADS) for c in range(n_chunks)]

    q_in, v_bf, decay, attn, kv = {}, {}, {}, {}, {}
    for hd, c in pairs:
        r0 = c * chunk
        bc = b[r0:r0 + chunk, hd * dk:(hd + 1) * dk]
        qc = z_ref[r0:r0 + chunk, o_q + hd * dk:o_q + (hd + 1) * dk]
        kc = z_ref[r0:r0 + chunk, o_k + hd * dk:o_k + (hd + 1) * dk]
        v_bf[hd, c] = z_ref[r0:r0 + chunk, o_v + hd * dv:o_v + (hd + 1) * dv].astype(BF16)
        b_last = bc[chunk - 1:chunk, :]
        b_mid = bc[chunk // 2 - 1:chunk // 2, :]
        qd = qc * jnp.exp(bc) * scale
        q_mid = (qd * jnp.exp(-b_mid)).astype(BF16)
        k_mid = kc * jnp.exp(b_mid - bc)
        k_end = (kc * jnp.exp(b_last - bc)).astype(BF16)
        q_in[hd, c] = qd.astype(BF16)
        decay[hd, c] = jnp.broadcast_to(jnp.exp(b_last), (dk, dk)).T
        attn[hd, c] = lax.dot_general(q_mid, k_mid.astype(BF16), nt_dims,
                                      preferred_element_type=F32)
        kv[hd, c] = lax.dot_general(k_end, v_bf[hd, c], tn_dims, preferred_element_type=F32)
    yield

    outs = {}
    for hd in range(GLA_HEADS):
        state = s_ref[hd]
        for c in range(n_chunks):
            a = jnp.where(causal, attn[hd, c], 0.0).astype(BF16)
            outs[hd, c] = (jnp.dot(q_in[hd, c], state.astype(BF16), preferred_element_type=F32)
                           + jnp.dot(a, v_bf[hd, c], preferred_element_type=F32))
            state = (state * jnp.concatenate([decay[hd, c]] * (dv // dk), axis=1) + kv[hd, c])
        s_ref[hd] = state
    yield

    for hd, c in pairs:
        r0 = c * chunk
        o = outs[hd, c]
        gout = z_ref[r0:r0 + chunk, o_g + hd * dv:o_g + (hd + 1) * dv]
        on = o * lax.rsqrt(jnp.mean(o * o, axis=-1, keepdims=True) + EPS) * gng
        yb = on * (gout * jax.nn.sigmoid(gout))
        mix_ref[r0:r0 + chunk, sc + hd * dv:sc + (hd + 1) * dv] = yb.astype(BF16)
    result.append(jnp.dot(mix_ref[...], wout_ref[...], preferred_element_type=F32))


def _even_exact_tile(x_ref, gpost_ref, gng_ref, wout_ref, mgpre_ref, x1buf_ref, hbuf_ref,
                     z_ref, mix_ref, bbuf_ref, sprev_ref, orow_ref, *, rows, chunk, d_model):
    sc = d_model // 2
    dk = d_model // 2 // GLA_HEADS
    dv = d_model // GLA_HEADS
    key = GLA_HEADS * dk
    o_q = 3 * sc
    o_k = o_q + key
    o_v = o_k + key
    o_g = o_v + GLA_HEADS * dv
    scale = dk ** -0.5
    tn_dims = (((0,), (0,)), ((), ()))
    gng = gng_ref[...]
    row_id = lax.broadcasted_iota(jnp.int32, (chunk, 1), 0)
    for hd in range(GLA_HEADS):
        klanes = slice(hd * dk, (hd + 1) * dk)
        qlanes = slice(o_q + hd * dk, o_q + (hd + 1) * dk)
        state = sprev_ref[hd]
        for c in range(rows // chunk):
            r0 = c * chunk
            bc = bbuf_ref[r0:r0 + chunk, klanes]
            kc = z_ref[r0:r0 + chunk, o_k + hd * dk:o_k + (hd + 1) * dk]
            vc = z_ref[r0:r0 + chunk, o_v + hd * dv:o_v + (hd + 1) * dv]
            gout = z_ref[r0:r0 + chunk, o_g + hd * dv:o_g + (hd + 1) * dv]
            b_last = bc[chunk - 1:chunk, :]
            q_in = (z_ref[r0:r0 + chunk, qlanes] * jnp.exp(bc) * scale).astype(BF16)
            k_end = (kc * jnp.exp(b_last - bc)).astype(BF16)

            def eight_rows(blk, carry):
                base = pl.multiple_of(blk * SUBLANES, SUBLANES)
                b_blk = bbuf_ref[pl.ds(r0 + base, SUBLANES), klanes]
                q_blk = z_ref[pl.ds(r0 + base, SUBLANES), qlanes] * scale
                out_rows = []
                for r in range(SUBLANES):
                    decay_ij = jnp.exp(jnp.minimum(b_blk[r:r + 1, :] - bc, 0.0))
                    w = (q_blk[r:r + 1, :] * decay_ij) * kc
                    a = jnp.where(row_id <= base + r, jnp.sum(w, axis=-1, keepdims=True), 0.0)
                    out_rows.append(jnp.sum(a * vc, axis=0, keepdims=True))
                orow_ref[pl.ds(base, SUBLANES), :] = jnp.concatenate(out_rows, axis=0)
                return carry

            lax.fori_loop(0, chunk // SUBLANES, eight_rows, 0)
            o = (jnp.dot(q_in, state.astype(BF16), preferred_element_type=F32) + orow_ref[...])
            decay = jnp.broadcast_to(jnp.exp(b_last), (dk, dk)).T
            state = (state * jnp.concatenate([decay] * (dv // dk), axis=1)
                     + lax.dot_general(k_end, vc.astype(BF16), tn_dims,
                                       preferred_element_type=F32))
            on = o * lax.rsqrt(jnp.mean(o * o, axis=-1, keepdims=True) + EPS) * gng
            yb = on * (gout * jax.nn.sigmoid(gout))
            mix_ref[r0:r0 + chunk, sc + hd * dv:sc + (hd + 1) * dv] = yb.astype(BF16)
    y = jnp.dot(mix_ref[...], wout_ref[...], preferred_element_type=F32)
    _store_mixer_output(x_ref[...] + _rms(y, gpost_ref[...]), mgpre_ref, x1buf_ref, hbuf_ref)


def _even_kernel(x_ref, gpre_ref, gpost_ref, win_ref, wgk2_ref, bgk2_ref, wsc_ref, gng_ref,
                 wout_hbm, mgpre_ref, mgpost_ref, w1_hbm, w2_hbm, o_ref,
                 x1buf_ref, hbuf_ref, z_ref, mix_ref, ubuf_ref, s_ref, bbuf_ref, sprev_ref,
                 orow_ref, wout_ref, w1_ref, w2_ref, stage_d_ref, stage_h_ref, sem_ref,
                 *, rows, chunk, d_model, layer, idx):
    @pl.when(pl.program_id(1) == 0)
    def _():
        x1buf_ref[...] = jnp.zeros_like(x1buf_ref)
        hbuf_ref[...] = jnp.zeros_like(hbuf_ref)
        s_ref[...] = jnp.zeros_like(s_ref)
        ubuf_ref[0:SUBLANES, :] = jnp.zeros((SUBLANES, ubuf_ref.shape[1]), F32)
        _load_weight_as_bf16(w1_hbm.at[layer], w1_ref, stage_h_ref, sem_ref)
        _load_weight_as_bf16(w2_hbm.at[layer], w2_ref, stage_d_ref, sem_ref)
        _load_weight_as_bf16(wout_hbm.at[idx], wout_ref, stage_d_ref, sem_ref)

    mixed, mlp_done = [], []
    _interleave(
        _mlp_stage(x1buf_ref, hbuf_ref[...], mgpost_ref, w1_ref, w2_ref, o_ref, mlp_done,
                   tail_parts=2),
        _even_mixer_stage(x_ref, gpre_ref, win_ref, wgk2_ref, bgk2_ref, wsc_ref, gng_ref,
                          wout_ref, z_ref, mix_ref, ubuf_ref, s_ref, bbuf_ref, sprev_ref, mixed,
                          rows=rows, chunk=chunk, d_model=d_model),
        head=EVEN_HEAD_ORDER)
    min_chunk_decay, y = mixed
    _store_mixer_output(x_ref[...] + _rms(y, gpost_ref[...]), mgpre_ref, x1buf_ref, hbuf_ref)

    @pl.when(min_chunk_decay < -GLA_FAST_PATH_MAX_DECAY)
    def _():
        _even_exact_tile(x_ref, gpost_ref, gng_ref, wout_ref, mgpre_ref, x1buf_ref, hbuf_ref,
                         z_ref, mix_ref, bbuf_ref, sprev_ref, orow_ref,
                         rows=rows, chunk=chunk, d_model=d_model)


def _tile_specs(rows, d, n_tiles):
    x_spec = pl.BlockSpec((None, rows, d), lambda b, s: (b, jnp.minimum(s, n_tiles - 1), 0))
    o_spec = pl.BlockSpec((None, rows, d), lambda b, s: (b, jnp.maximum(s - 1, 0), 0))
    return x_spec, o_spec


def _even_call(x3d, g_pre, g_post, w_in, w_gk2, b_gk2, w_sc, gn_g, w_out,
               mg_pre, mg_post, w1, w2, layer, idx):
    bsz, seq, d = x3d.shape
    rows, chunk = TILE_ROWS, GLA_CHUNK
    assert seq % rows == 0 and rows % chunk == 0
    n_tiles = seq // rows
    n_in = w_in.shape[-1]
    key = w_gk2.shape[-1]
    sc = w_sc.shape[-1]
    mix = w_out.shape[1]
    hidden = w1.shape[-1]
    dk = key // GLA_HEADS
    dv = d // GLA_HEADS
    x_spec, o_spec = _tile_specs(rows, d, n_tiles)
    hbm = pl.BlockSpec(memory_space=pl.ANY)
    kern = functools.partial(_even_kernel, rows=rows, chunk=chunk, d_model=d,
                             layer=layer, idx=idx)
    return pl.pallas_call(
        kern,
        grid=(bsz, n_tiles + 1),
        in_specs=[
            x_spec,
            _const_spec((None, 1, d), lambda b, s: (layer, 0, 0)),
            _const_spec((None, 1, d), lambda b, s: (layer, 0, 0)),
            _const_spec((None, d, n_in), lambda b, s: (idx, 0, 0)),
            _const_spec((None, LANES, key), lambda b, s: (idx, 0, 0)),
            _const_spec((None, 1, key), lambda b, s: (idx, 0, 0)),
            _const_spec((None, SC_KERNEL, sc), lambda b, s: (idx, 0, 0)),
            _const_spec((None, 1, dv), lambda b, s: (idx, 0, 0)),
            hbm,
            _const_spec((None, 1, d), lambda b, s: (layer, 0, 0)),
            _const_spec((None, 1, d), lambda b, s: (layer, 0, 0)),
            hbm,
            hbm,
        ],
        out_specs=o_spec,
        out_shape=jax.ShapeDtypeStruct((bsz, seq, d), F32),
        scratch_shapes=[
            pltpu.VMEM((rows, d), F32),
            pltpu.VMEM((rows, d), BF16),
            pltpu.VMEM((rows, n_in), F32),
            pltpu.VMEM((rows, mix), BF16),
            pltpu.VMEM((rows + SUBLANES, sc), F32),
            pltpu.VMEM((GLA_HEADS, dk, dv), F32),
            pltpu.VMEM((rows, key), F32),
            pltpu.VMEM((GLA_HEADS, dk, dv), F32),
            pltpu.VMEM((chunk, dv), F32),
            pltpu.VMEM((mix, d), BF16),
            pltpu.VMEM((d, hidden), BF16),
            pltpu.VMEM((hidden, d), BF16),
            pltpu.VMEM((2, _stage_rows(d), d), F32),
            pltpu.VMEM((2, _stage_rows(hidden), hidden), F32),
            pltpu.SemaphoreType.DMA((2,)),
        ],
        compiler_params=pltpu.CompilerParams(
            dimension_semantics=("arbitrary", "arbitrary"),
            vmem_limit_bytes=VMEM_LIMIT_BYTES),
        name="shortconv_gla_layer",
    )(x3d, g_pre, g_post, w_in, w_gk2, b_gk2, w_sc, gn_g, w_out, mg_pre, mg_post, w1, w2)


def _depthwise_conv(ubuf_ref, wdw_ref, bdw_ref, dbuf_ref, *, rows, d_model):
    sub = SUBLANES
    halo_blocks = CF_HALO // sub
    n_blocks = rows // sub
    max_a = (CF_KERNEL - 1) // sub
    assert max_a < halo_blocks
    row_id = lax.broadcasted_iota(jnp.int32, (sub, LANES), 0)
    for lg in range(d_model // LANES):
        lanes = slice(lg * LANES, (lg + 1) * LANES)
        w = {}
        for s in range(CF_KERNEL):
            w[s] = jnp.broadcast_to(wdw_ref[CF_KERNEL - 1 - s:CF_KERNEL - s, lanes], (sub, LANES))
        bias = jnp.broadcast_to(bdw_ref[:, lanes], (sub, LANES))
        blocks = {}

        def block(m):
            if m not in blocks:
                blocks[m] = ubuf_ref[m * sub:(m + 1) * sub, lanes]
            return blocks[m]

        prev_q = None
        for m in range(halo_blocks - 1, halo_blocks + n_blocks):
            q = []
            for r in range(sub):
                acc = None
                for a in range(max_a + 1):
                    s = sub * a + r
                    if s >= CF_KERNEL:
                        continue
                    term = w[s] * block(m - a)
                    acc = term if acc is None else acc + term
                q.append(acc)
            if m >= halo_blocks:
                out = bias + q[0]
                for r in range(1, sub):
                    both = jnp.where(row_id < sub - r, q[r], prev_q[r])
                    out = out + pltpu.roll(both, r, axis=0)
                k = m - halo_blocks
                dbuf_ref[k * sub:(k + 1) * sub, lanes] = out
            prev_q = q
            blocks.pop(m - max_a, None)
        yield


def _odd_kernel(x_ref, gpre_ref, gpost_ref, wpw1_hbm, bpw1_ref, wdw_ref, bdw_ref,
                lng_ref, lnb_ref, wpw2_hbm, bpw2_ref, mgpre_ref, mgpost_ref, w1_hbm, w2_hbm,
                o_ref, x1buf_ref, ubuf_ref, dbuf_ref, wpw1_ref, wpw2_ref, w1_ref, w2_ref,
                stage_d_ref, stage_2d_ref, stage_h_ref, sem_ref, *, rows, d_model, layer, idx):
    @pl.when(pl.program_id(1) == 0)
    def _():
        x1buf_ref[...] = jnp.zeros_like(x1buf_ref)
        ubuf_ref[0:CF_HALO, :] = jnp.zeros((CF_HALO, d_model), F32)
        _load_weight_as_bf16(wpw1_hbm.at[idx], wpw1_ref, stage_2d_ref, sem_ref)
        _load_weight_as_bf16(w1_hbm.at[layer], w1_ref, stage_h_ref, sem_ref)
        _load_weight_as_bf16(w2_hbm.at[layer], w2_ref, stage_d_ref, sem_ref)
        _load_weight_as_bf16(wpw2_hbm.at[idx], wpw2_ref, stage_d_ref, sem_ref)

    h = _rms(x_ref[...], gpre_ref[...]).astype(BF16)
    u = jnp.dot(h, wpw1_ref[...], preferred_element_type=F32) + bpw1_ref[...]
    ubuf_ref[CF_HALO:CF_HALO + rows, :] = u[:, :d_model] * jax.nn.sigmoid(u[:, d_model:])

    mlp_done = []
    h_mlp = _rms(x1buf_ref[...], mgpre_ref[...]).astype(BF16)
    _interleave(
        _mlp_stage(x1buf_ref, h_mlp, mgpost_ref, w1_ref, w2_ref, o_ref, mlp_done, tail_parts=1),
        _depthwise_conv(ubuf_ref, wdw_ref, bdw_ref, dbuf_ref, rows=rows, d_model=d_model))
    ubuf_ref[0:CF_HALO, :] = ubuf_ref[rows:rows + CF_HALO, :]

    _order_after(dbuf_ref, (slice(0, SUBLANES), slice(0, LANES)), mlp_done[-1])
    d = dbuf_ref[...]
    mu = jnp.mean(d, axis=-1, keepdims=True)
    dc = d - mu
    dn = dc * lax.rsqrt(jnp.mean(dc * dc, axis=-1, keepdims=True) + EPS)
    dn = dn * lng_ref[...] + lnb_ref[...]
    act = (dn * jax.nn.sigmoid(dn)).astype(BF16)
    y = jnp.dot(act, wpw2_ref[...], preferred_element_type=F32) + bpw2_ref[...]
    x1buf_ref[...] = x_ref[...] + _rms(y, gpost_ref[...])


def _odd_call(x3d, g_pre, g_post, w_pw1, b_pw1, w_dw, b_dw, ln_g, ln_b, w_pw2, b_pw2,
              mg_pre, mg_post, w1, w2, layer, idx):
    bsz, seq, d = x3d.shape
    rows = TILE_ROWS
    assert seq % rows == 0 and rows % SUBLANES == 0 and rows >= CF_HALO
    n_tiles = seq // rows
    hidden = w1.shape[-1]
    x_spec, o_spec = _tile_specs(rows, d, n_tiles)
    hbm = pl.BlockSpec(memory_space=pl.ANY)
    kern = functools.partial(_odd_kernel, rows=rows, d_model=d, layer=layer, idx=idx)
    return pl.pallas_call(
        kern,
        grid=(bsz, n_tiles + 1),
        in_specs=[
            x_spec,
            _const_spec((None, 1, d), lambda b, s: (layer, 0, 0)),
            _const_spec((None, 1, d), lambda b, s: (layer, 0, 0)),
            hbm,
            _const_spec((None, 1, 2 * d), lambda b, s: (idx, 0, 0)),
            _const_spec((None, CF_KERNEL, d), lambda b, s: (idx, 0, 0)),
            _const_spec((None, 1, d), lambda b, s: (idx, 0, 0)),
            _const_spec((None, 1, d), lambda b, s: (idx, 0, 0)),
            _const_spec((None, 1, d), lambda b, s: (idx, 0, 0)),
            hbm,
            _const_spec((None, 1, d), lambda b, s: (idx, 0, 0)),
            _const_spec((None, 1, d), lambda b, s: (layer, 0, 0)),
            _const_spec((None, 1, d), lambda b, s: (layer, 0, 0)),
            hbm,
            hbm,
        ],
        out_specs=o_spec,
        out_shape=jax.ShapeDtypeStruct((bsz, seq, d), F32),
        scratch_shapes=[
            pltpu.VMEM((rows, d), F32),
            pltpu.VMEM((rows + CF_HALO, d), F32),
            pltpu.VMEM((rows, d), F32),
            pltpu.VMEM((d, 2 * d), BF16),
            pltpu.VMEM((d, d), BF16),
            pltpu.VMEM((d, hidden), BF16),
            pltpu.VMEM((hidden, d), BF16),
            pltpu.VMEM((2, _stage_rows(d), d), F32),
            pltpu.VMEM((2, _stage_rows(2 * d), 2 * d), F32),
            pltpu.VMEM((2, _stage_rows(hidden), hidden), F32),
            pltpu.SemaphoreType.DMA((2,)),
        ],
        compiler_params=pltpu.CompilerParams(
            dimension_semantics=("arbitrary", "arbitrary"),
            vmem_limit_bytes=VMEM_LIMIT_BYTES),
        name="conformer_layer",
    )(x3d, g_pre, g_post, w_pw1, b_pw1, w_dw, b_dw, ln_g, ln_b, w_pw2, b_pw2,
      mg_pre, mg_post, w1, w2)


def kernel(x, norm_mix_pre, norm_mix_post, norm_mlp_pre, norm_mlp_post,
           ab_w_in, gla_w_gk2, gla_b_gk2, sc_w_conv, gla_norm_g, ab_w_out,
           cf_w_pw1, cf_b_pw1, cf_w_dw, cf_b_dw, cf_ln_g, cf_ln_b, cf_w_pw2, cf_b_pw2,
           mlp_w1, mlp_w2):
    depth = norm_mix_pre.shape[0]
    n_in = ab_w_in.shape[-1]
    n_main = n_in - GLA_RANK
    assert n_main % LANES == 0 and mlp_w1.shape[-1] % MLP_HIDDEN_CHUNK == 0

    n_pad = -(-(n_main + LANES) // MXU_DIM) * MXU_DIM
    w_in = jnp.pad(ab_w_in.astype(BF16), ((0, 0), (0, 0), (0, n_pad - n_in)))
    w_gk2 = jnp.pad(gla_w_gk2.astype(BF16), ((0, 0), (0, LANES - GLA_RANK), (0, 0)))
    w_out, w_pw1, w_pw2, w1, w2 = ab_w_out, cf_w_pw1, cf_w_pw2, mlp_w1, mlp_w2

    def rows3d(v):
        return v.reshape(v.shape[0], 1, v.shape[1])

    norm_mix_pre, norm_mix_post, norm_mlp_pre, norm_mlp_post = map(
        rows3d, (norm_mix_pre, norm_mix_post, norm_mlp_pre, norm_mlp_post))
    gla_b_gk2, gla_norm_g = rows3d(gla_b_gk2), rows3d(gla_norm_g)
    cf_b_pw1, cf_b_dw, cf_ln_g, cf_ln_b, cf_b_pw2 = map(
        rows3d, (cf_b_pw1, cf_b_dw, cf_ln_g, cf_ln_b, cf_b_pw2))

    for layer in range(depth):
        idx = layer // 2
        if layer % 2 == 0:
            x = _even_call(x, norm_mix_pre, norm_mix_post, w_in, w_gk2, gla_b_gk2,
                           sc_w_conv, gla_norm_g, w_out, norm_mlp_pre, norm_mlp_post,
                           w1, w2, layer, idx)
        else:
            x = _odd_call(x, norm_mix_pre, norm_mix_post, w_pw1, cf_b_pw1, cf_w_dw,
                          cf_b_dw, cf_ln_g, cf_ln_b, w_pw2, cf_b_pw2, norm_mlp_pre,
                          norm_mlp_post, w1, w2, layer, idx)
    return x
```

```python
import functools

import jax
import jax.numpy as jnp
from jax import lax
from jax.experimental import pallas as pl
from jax.experimental.pallas import tpu as pltpu

F32 = jnp.float32
BF16 = jnp.bfloat16

EPS = 1e-6
GLA_HEADS = 4
GLA_RANK = 16
GLA_GATE_NORM = 16.0
SC_KERNEL = 3
CF_KERNEL = 31

LANES = 128
SUBLANES = 8
VMEM_LIMIT_BYTES = 56 * 1024 * 1024
WEIGHT_STAGE_BYTES = 1024 * 1024

TILE_ROWS = 256
MLP_HIDDEN_CHUNK = 1024
GLA_CHUNK = 128
EVEN_HEAD_ORDER = (0, 0, 0)
GLA_FAST_PATH_MAX_DECAY = 60.0
CF_HALO = 32
def _rms(x, g):
    ms = jnp.mean(x * x, axis=-1, keepdims=True)
    return x * lax.rsqrt(ms + EPS) * g


def _const_spec(shape, index_map):
    return pl.BlockSpec(shape, index_map, pipeline_mode=pl.Buffered(1))


def _zero_bits_of(v):
    u = lax.bitcast_convert_type(v, jnp.uint32)
    return lax.shift_right_logical(lax.shift_right_logical(u, jnp.uint32(16)), jnp.uint32(16))


def _order_after(ref, block, token):
    bits = lax.bitcast_convert_type(ref[block], jnp.uint32) | token
    ref[block] = lax.bitcast_convert_type(bits, F32)


def _load_weight_as_bf16(src_hbm_ref, dst_ref, stage_ref, sem_ref):
    k_rows, n_cols = dst_ref.shape
    chunk = stage_ref.shape[1]
    assert k_rows % chunk == 0 and stage_ref.shape[2] == n_cols
    n_chunks = k_rows // chunk

    def copy(i):
        return pltpu.make_async_copy(
            src_hbm_ref.at[pl.ds(i * chunk, chunk), pl.ds(0, n_cols)],
            stage_ref.at[i % 2], sem_ref.at[i % 2])

    copy(0).start()
    for i in range(n_chunks):
        if i + 1 < n_chunks:
            copy(i + 1).start()
        copy(i).wait()
        dst_ref[i * chunk:(i + 1) * chunk, :] = stage_ref[i % 2].astype(BF16)


def _stage_rows(n_cols):
    rows = WEIGHT_STAGE_BYTES // (4 * n_cols)
    return 1 << (rows.bit_length() - 1)


def _store_mixer_output(x1, mgpre_ref, x1buf_ref, hbuf_ref):
    x1buf_ref[...] = x1
    hbuf_ref[...] = _rms(x1, mgpre_ref[...]).astype(BF16)


def _mlp_stage(x1buf_ref, h, gpost_ref, w1_ref, w2_ref, o_ref, done, *, tail_parts):
    rows = h.shape[0]
    part = rows // tail_parts
    hidden = w1_ref.shape[1]
    n_chunks = hidden // MLP_HIDDEN_CHUNK

    def cols(c):
        return slice(c * MLP_HIDDEN_CHUNK, (c + 1) * MLP_HIDDEN_CHUNK)

    def up(c):
        raw = jnp.dot(h, w1_ref[:, cols(c)], preferred_element_type=F32)
        raw = jnp.maximum(raw, 0.0)
        return (raw * raw).astype(BF16)

    acc = None
    act = up(0)
    yield
    for c in range(1, n_chunks):
        act_next = up(c)
        yield
        p = jnp.dot(act, w2_ref[cols(c - 1), :], preferred_element_type=F32)
        acc = p if acc is None else acc + p
        act = act_next
        yield
    out = None
    for t in range(tail_parts):
        sl = slice(t * part, (t + 1) * part)
        p = jnp.dot(act[sl], w2_ref[cols(n_chunks - 1), :], preferred_element_type=F32)
        total = p if acc is None else acc[sl] + p
        out = x1buf_ref[sl, :] + _rms(total, gpost_ref[...])
        o_ref[sl, :] = out
    done.append(_zero_bits_of(out[0:SUBLANES, 0:LANES]))


def _interleave(*gens, head=()):
    live = list(gens)

    def step(g):
        try:
            next(g)
        except StopIteration:
            live.remove(g)

    for i in head:
        step(gens[i])
    while live:
        for g in list(live):
            step(g)


def _even_mixer_stage(x_ref, gpre_ref, win_ref, wgk1_ref, wgk2_ref, bgk2_ref, wsc_ref, gng_ref,
                      wout_ref,
                      z_ref, mix_ref, ubuf_ref, s_ref, bbuf_ref, sprev_ref, result,
                      *, rows, chunk, d_model):
    sc = d_model // 2
    dk = d_model // 2 // GLA_HEADS
    dv = d_model // GLA_HEADS
    key = GLA_HEADS * dk
    o_scx, o_scb, o_scc = 0, sc, 2 * sc
    o_q = 3 * sc
    o_k = o_q + key
    o_v = o_k + key
    o_g = o_v + GLA_HEADS * dv

    h = _rms(x_ref[...], gpre_ref[...]).astype(BF16)
    z_ref[...] = jnp.dot(h, win_ref[...], preferred_element_type=F32)
    gk_lr = jnp.dot(h, wgk1_ref[...], preferred_element_type=F32).astype(BF16)
    yield

    gk = jnp.dot(gk_lr, wgk2_ref[...], preferred_element_type=F32) + bgk2_ref[...]
    yield
    g = (jnp.minimum(gk, 0.0) - jnp.log1p(jnp.exp(-jnp.abs(gk)))) * (1.0 / GLA_GATE_NORM)
    g_hi = g.astype(BF16)
    g_lo = (g - g_hi.astype(F32)).astype(BF16)
    ri = lax.broadcasted_iota(jnp.int32, (rows, rows), 0)
    ci = lax.broadcasted_iota(jnp.int32, (rows, rows), 1)
    same_chunk = (ri // chunk) == (ci // chunk)
    tril_chunks = jnp.where(same_chunk & (ci <= ri), 1.0, 0.0).astype(BF16)
    b = (jnp.dot(tril_chunks, g_hi, preferred_element_type=F32)
         + jnp.dot(tril_chunks, g_lo, preferred_element_type=F32))
    bbuf_ref[...] = b
    sprev_ref[...] = s_ref[...]
    n_chunks = rows // chunk
    chunk_totals = jnp.concatenate(
        [b[(c + 1) * chunk - 1:(c + 1) * chunk, :] for c in range(n_chunks)], axis=0)
    result.append(jnp.min(chunk_totals))
    yield

    u = z_ref[:, o_scc:o_scc + sc] * z_ref[:, o_scx:o_scx + sc]
    ubuf_ref[SUBLANES:SUBLANES + rows, :] = u
    conv = wsc_ref[SC_KERNEL - 1:SC_KERNEL, :] * u
    for j in range(SC_KERNEL - 1):
        shift = SC_KERNEL - 1 - j
        conv = conv + wsc_ref[j:j + 1, :] * ubuf_ref[SUBLANES - shift:SUBLANES - shift + rows, :]
    ubuf_ref[0:SUBLANES, :] = u[rows - SUBLANES:rows, :]
    mix_ref[:, 0:sc] = (z_ref[:, o_scb:o_scb + sc] * conv).astype(BF16)

    causal = (lax.broadcasted_iota(jnp.int32, (chunk, chunk), 1)
              <= lax.broadcasted_iota(jnp.int32, (chunk, chunk), 0))
    scale = dk ** -0.5
    nt_dims = (((1,), (1,)), ((), ()))
    tn_dims = (((0,), (0,)), ((), ()))
    gng = gng_ref[...]
    pairs = [(hd, c) for hd in range(GLA_HEADS) for c in range(n_chunks)]

    q_in, v_bf, decay, attn, kv = {}, {}, {}, {}, {}
    for hd, c in pairs:
        r0 = c * chunk
        bc = b[r0:r0 + chunk, hd * dk:(hd + 1) * dk]
        qc = z_ref[r0:r0 + chunk, o_q + hd * dk:o_q + (hd + 1) * dk]
        kc = z_ref[r0:r0 + chunk, o_k + hd * dk:o_k + (hd + 1) * dk]
        v_bf[hd, c] = z_ref[r0:r0 + chunk, o_v + hd * dv:o_v + (hd + 1) * dv].astype(BF16)
        b_last = bc[chunk - 1:chunk, :]
        b_mid = bc[chunk // 2 - 1:chunk // 2, :]
        qd = qc * jnp.exp(bc) * scale
        q_mid = (qd * jnp.exp(-b_mid)).astype(BF16)
        k_mid = kc * jnp.exp(b_mid - bc)
        k_end = (kc * jnp.exp(b_last - bc)).astype(BF16)
        q_in[hd, c] = qd.astype(BF16)
        decay[hd, c] = jnp.broadcast_to(jnp.exp(b_last), (dk, dk)).T
        attn[hd, c] = lax.dot_general(q_mid, k_mid.astype(BF16), nt_dims,
                                      preferred_element_type=F32)
        kv[hd, c] = lax.dot_general(k_end, v_bf[hd, c], tn_dims, preferred_element_type=F32)
    yield

    outs = {}
    for hd in range(GLA_HEADS):
        state = s_ref[hd]
        for c in range(n_chunks):
            a = jnp.where(causal, attn[hd, c], 0.0).astype(BF16)
            outs[hd, c] = (jnp.dot(q_in[hd, c], state.astype(BF16), preferred_element_type=F32)
                           + jnp.dot(a, v_bf[hd, c], preferred_element_type=F32))
            state = (state * jnp.concatenate([decay[hd, c]] * (dv // dk), axis=1) + kv[hd, c])
        s_ref[hd] = state
    yield

    for hd, c in pairs:
        r0 = c * chunk
        o = outs[hd, c]
        gout = z_ref[r0:r0 + chunk, o_g + hd * dv:o_g + (hd + 1) * dv]
        on = o * lax.rsqrt(jnp.mean(o * o, axis=-1, keepdims=True) + EPS) * gng
        yb = on * (gout * jax.nn.sigmoid(gout))
        mix_ref[r0:r0 + chunk, sc + hd * dv:sc + (hd + 1) * dv] = yb.astype(BF16)
    result.append(jnp.dot(mix_ref[...], wout_ref[...], preferred_element_type=F32))


def _even_exact_tile(x_ref, gpost_ref, gng_ref, wout_ref, mgpre_ref, x1buf_ref, hbuf_ref,
                     z_ref, mix_ref, bbuf_ref, sprev_ref, orow_ref, *, rows, chunk, d_model):
    sc = d_model // 2
    dk = d_model // 2 // GLA_HEADS
    dv = d_model // GLA_HEADS
    key = GLA_HEADS * dk
    o_q = 3 * sc
    o_k = o_q + key
    o_v = o_k + key
    o_g = o_v + GLA_HEADS * dv
    scale = dk ** -0.5
    tn_dims = (((0,), (0,)), ((), ()))
    gng = gng_ref[...]
    row_id = lax.broadcasted_iota(jnp.int32, (chunk, 1), 0)
    for hd in range(GLA_HEADS):
        klanes = slice(hd * dk, (hd + 1) * dk)
        qlanes = slice(o_q + hd * dk, o_q + (hd + 1) * dk)
        state = sprev_ref[hd]
        for c in range(rows // chunk):
            r0 = c * chunk
            bc = bbuf_ref[r0:r0 + chunk, klanes]
            kc = z_ref[r0:r0 + chunk, o_k + hd * dk:o_k + (hd + 1) * dk]
            vc = z_ref[r0:r0 + chunk, o_v + hd * dv:o_v + (hd + 1) * dv]
            gout = z_ref[r0:r0 + chunk, o_g + hd * dv:o_g + (hd + 1) * dv]
            b_last = bc[chunk - 1:chunk, :]
            q_in = (z_ref[r0:r0 + chunk, qlanes] * jnp.exp(bc) * scale).astype(BF16)
            k_end = (kc * jnp.exp(b_last - bc)).astype(BF16)

            def eight_rows(blk, carry):
                base = pl.multiple_of(blk * SUBLANES, SUBLANES)
                b_blk = bbuf_ref[pl.ds(r0 + base, SUBLANES), klanes]
                q_blk = z_ref[pl.ds(r0 + base, SUBLANES), qlanes] * scale
                out_rows = []
                for r in range(SUBLANES):
                    decay_ij = jnp.exp(jnp.minimum(b_blk[r:r + 1, :] - bc, 0.0))
                    w = (q_blk[r:r + 1, :] * decay_ij) * kc
                    a = jnp.where(row_id <= base + r, jnp.sum(w, axis=-1, keepdims=True), 0.0)
                    out_rows.append(jnp.sum(a * vc, axis=0, keepdims=True))
                orow_ref[pl.ds(base, SUBLANES), :] = jnp.concatenate(out_rows, axis=0)
                return carry

            lax.fori_loop(0, chunk // SUBLANES, eight_rows, 0)
            o = (jnp.dot(q_in, state.astype(BF16), preferred_element_type=F32) + orow_ref[...])
            decay = jnp.broadcast_to(jnp.exp(b_last), (dk, dk)).T
            state = (state * jnp.concatenate([decay] * (dv // dk), axis=1)
                     + lax.dot_general(k_end, vc.astype(BF16), tn_dims,
                                       preferred_element_type=F32))
            on = o * lax.rsqrt(jnp.mean(o * o, axis=-1, keepdims=True) + EPS) * gng
            yb = on * (gout * jax.nn.sigmoid(gout))
            mix_ref[r0:r0 + chunk, sc + hd * dv:sc + (hd + 1) * dv] = yb.astype(BF16)
    y = jnp.dot(mix_ref[...], wout_ref[...], preferred_element_type=F32)
    _store_mixer_output(x_ref[...] + _rms(y, gpost_ref[...]), mgpre_ref, x1buf_ref, hbuf_ref)


def _even_kernel(x_ref, gpre_ref, gpost_ref, win_hbm, wgk1_ref, wgk2_ref, bgk2_ref, wsc_ref,
                 gng_ref, wout_hbm, mgpre_ref, mgpost_ref, w1_hbm, w2_hbm, o_ref,
                 x1buf_ref, hbuf_ref, z_ref, mix_ref, ubuf_ref, s_ref, bbuf_ref, sprev_ref,
                 orow_ref, win_ref, wout_ref, w1_ref, w2_ref, stage_in_ref, stage_d_ref,
                 stage_h_ref, sem_ref, *, rows, chunk, d_model, layer, idx):
    @pl.when(pl.program_id(1) == 0)
    def _():
        x1buf_ref[...] = jnp.zeros_like(x1buf_ref)
        hbuf_ref[...] = jnp.zeros_like(hbuf_ref)
        s_ref[...] = jnp.zeros_like(s_ref)
        ubuf_ref[0:SUBLANES, :] = jnp.zeros((SUBLANES, ubuf_ref.shape[1]), F32)
        _load_weight_as_bf16(w1_hbm.at[layer], w1_ref, stage_h_ref, sem_ref)
        _load_weight_as_bf16(w2_hbm.at[layer], w2_ref, stage_d_ref, sem_ref)
        _load_weight_as_bf16(win_hbm.at[idx], win_ref, stage_in_ref, sem_ref)
        _load_weight_as_bf16(wout_hbm.at[idx], wout_ref, stage_d_ref, sem_ref)

    mixed, mlp_done = [], []
    _interleave(
        _mlp_stage(x1buf_ref, hbuf_ref[...], mgpost_ref, w1_ref, w2_ref, o_ref, mlp_done,
                   tail_parts=2),
        _even_mixer_stage(x_ref, gpre_ref, win_ref, wgk1_ref, wgk2_ref, bgk2_ref, wsc_ref,
                          gng_ref, wout_ref, z_ref, mix_ref, ubuf_ref, s_ref, bbuf_ref, sprev_ref, mixed,
                          rows=rows, chunk=chunk, d_model=d_model),
        head=EVEN_HEAD_ORDER)
    min_chunk_decay, y = mixed
    _store_mixer_output(x_ref[...] + _rms(y, gpost_ref[...]), mgpre_ref, x1buf_ref, hbuf_ref)

    @pl.when(min_chunk_decay < -GLA_FAST_PATH_MAX_DECAY)
    def _():
        _even_exact_tile(x_ref, gpost_ref, gng_ref, wout_ref, mgpre_ref, x1buf_ref, hbuf_ref,
                         z_ref, mix_ref, bbuf_ref, sprev_ref, orow_ref,
                         rows=rows, chunk=chunk, d_model=d_model)


def _tile_specs(rows, d, n_tiles):
    x_spec = pl.BlockSpec((None, rows, d), lambda b, s: (b, jnp.minimum(s, n_tiles - 1), 0))
    o_spec = pl.BlockSpec((None, rows, d), lambda b, s: (b, jnp.maximum(s - 1, 0), 0))
    return x_spec, o_spec


def _even_call(x3d, g_pre, g_post, w_in, w_gk1, w_gk2, b_gk2, w_sc, gn_g, w_out,
               mg_pre, mg_post, w1, w2, layer, idx):
    bsz, seq, d = x3d.shape
    rows, chunk = TILE_ROWS, GLA_CHUNK
    assert seq % rows == 0 and rows % chunk == 0
    n_tiles = seq // rows
    n_main = w_in.shape[-1] - GLA_RANK
    key = w_gk2.shape[-1]
    sc = w_sc.shape[-1]
    mix = w_out.shape[1]
    hidden = w1.shape[-1]
    dk = key // GLA_HEADS
    dv = d // GLA_HEADS
    x_spec, o_spec = _tile_specs(rows, d, n_tiles)
    hbm = pl.BlockSpec(memory_space=pl.ANY)
    kern = functools.partial(_even_kernel, rows=rows, chunk=chunk, d_model=d,
                             layer=layer, idx=idx)
    return pl.pallas_call(
        kern,
        grid=(bsz, n_tiles + 1),
        in_specs=[
            x_spec,
            _const_spec((None, 1, d), lambda b, s: (layer, 0, 0)),
            _const_spec((None, 1, d), lambda b, s: (layer, 0, 0)),
            hbm,
            _const_spec((None, d, LANES), lambda b, s: (idx, 0, 0)),
            _const_spec((None, LANES, key), lambda b, s: (idx, 0, 0)),
            _const_spec((None, 1, key), lambda b, s: (idx, 0, 0)),
            _const_spec((None, SC_KERNEL, sc), lambda b, s: (idx, 0, 0)),
            _const_spec((None, 1, dv), lambda b, s: (idx, 0, 0)),
            hbm,
            _const_spec((None, 1, d), lambda b, s: (layer, 0, 0)),
            _const_spec((None, 1, d), lambda b, s: (layer, 0, 0)),
            hbm,
            hbm,
        ],
        out_specs=o_spec,
        out_shape=jax.ShapeDtypeStruct((bsz, seq, d), F32),
        scratch_shapes=[
            pltpu.VMEM((rows, d), F32),
            pltpu.VMEM((rows, d), BF16),
            pltpu.VMEM((rows, n_main), F32),
            pltpu.VMEM((rows, mix), BF16),
            pltpu.VMEM((rows + SUBLANES, sc), F32),
            pltpu.VMEM((GLA_HEADS, dk, dv), F32),
            pltpu.VMEM((rows, key), F32),
            pltpu.VMEM((GLA_HEADS, dk, dv), F32),
            pltpu.VMEM((chunk, dv), F32),
            pltpu.VMEM((d, n_main), BF16),
            pltpu.VMEM((mix, d), BF16),
            pltpu.VMEM((d, hidden), BF16),
            pltpu.VMEM((hidden, d), BF16),
            pltpu.VMEM((2, _stage_rows(n_main), n_main), F32),
            pltpu.VMEM((2, _stage_rows(d), d), F32),
            pltpu.VMEM((2, _stage_rows(hidden), hidden), F32),
            pltpu.SemaphoreType.DMA((2,)),
        ],
        compiler_params=pltpu.CompilerParams(
            dimension_semantics=("arbitrary", "arbitrary"),
            vmem_limit_bytes=VMEM_LIMIT_BYTES),
        name="shortconv_gla_layer",
    )(x3d, g_pre, g_post, w_in, w_gk1, w_gk2, b_gk2, w_sc, gn_g, w_out, mg_pre, mg_post, w1, w2)


def _depthwise_conv(ubuf_ref, wdw_ref, bdw_ref, dbuf_ref, *, rows, d_model):
    sub = SUBLANES
    halo_blocks = CF_HALO // sub
    n_blocks = rows // sub
    max_a = (CF_KERNEL - 1) // sub
    assert max_a < halo_blocks
    row_id = lax.broadcasted_iota(jnp.int32, (sub, LANES), 0)
    for lg in range(d_model // LANES):
        lanes = slice(lg * LANES, (lg + 1) * LANES)
        w = {}
        for s in range(CF_KERNEL):
            w[s] = jnp.broadcast_to(wdw_ref[CF_KERNEL - 1 - s:CF_KERNEL - s, lanes], (sub, LANES))
        bias = jnp.broadcast_to(bdw_ref[:, lanes], (sub, LANES))
        blocks = {}

        def block(m):
            if m not in blocks:
                blocks[m] = ubuf_ref[m * sub:(m + 1) * sub, lanes]
            return blocks[m]

        prev_q = None
        for m in range(halo_blocks - 1, halo_blocks + n_blocks):
            q = []
            for r in range(sub):
                acc = None
                for a in range(max_a + 1):
                    s = sub * a + r
                    if s >= CF_KERNEL:
                        continue
                    term = w[s] * block(m - a)
                    acc = term if acc is None else acc + term
                q.append(acc)
            if m >= halo_blocks:
                out = bias + q[0]
                for r in range(1, sub):
                    both = jnp.where(row_id < sub - r, q[r], prev_q[r])
                    out = out + pltpu.roll(both, r, axis=0)
                k = m - halo_blocks
                dbuf_ref[k * sub:(k + 1) * sub, lanes] = out
            prev_q = q
            blocks.pop(m - max_a, None)
        yield


def _odd_kernel(x_ref, gpre_ref, gpost_ref, wpw1_hbm, bpw1_ref, wdw_ref, bdw_ref,
                lng_ref, lnb_ref, wpw2_hbm, bpw2_ref, mgpre_ref, mgpost_ref, w1_hbm, w2_hbm,
                o_ref, x1buf_ref, ubuf_ref, dbuf_ref, wpw1_ref, wpw2_ref, w1_ref, w2_ref,
                stage_d_ref, stage_2d_ref, stage_h_ref, sem_ref, *, rows, d_model, layer, idx):
    @pl.when(pl.program_id(1) == 0)
    def _():
        x1buf_ref[...] = jnp.zeros_like(x1buf_ref)
        ubuf_ref[0:CF_HALO, :] = jnp.zeros((CF_HALO, d_model), F32)
        _load_weight_as_bf16(wpw1_hbm.at[idx], wpw1_ref, stage_2d_ref, sem_ref)
        _load_weight_as_bf16(w1_hbm.at[layer], w1_ref, stage_h_ref, sem_ref)
        _load_weight_as_bf16(w2_hbm.at[layer], w2_ref, stage_d_ref, sem_ref)
        _load_weight_as_bf16(wpw2_hbm.at[idx], wpw2_ref, stage_d_ref, sem_ref)

    h = _rms(x_ref[...], gpre_ref[...]).astype(BF16)
    u = jnp.dot(h, wpw1_ref[...], preferred_element_type=F32) + bpw1_ref[...]
    ubuf_ref[CF_HALO:CF_HALO + rows, :] = u[:, :d_model] * jax.nn.sigmoid(u[:, d_model:])

    mlp_done = []
    h_mlp = _rms(x1buf_ref[...], mgpre_ref[...]).astype(BF16)
    _interleave(
        _mlp_stage(x1buf_ref, h_mlp, mgpost_ref, w1_ref, w2_ref, o_ref, mlp_done, tail_parts=1),
        _depthwise_conv(ubuf_ref, wdw_ref, bdw_ref, dbuf_ref, rows=rows, d_model=d_model))
    ubuf_ref[0:CF_HALO, :] = ubuf_ref[rows:rows + CF_HALO, :]

    _order_after(dbuf_ref, (slice(0, SUBLANES), slice(0, LANES)), mlp_done[-1])
    d = dbuf_ref[...]
    mu = jnp.mean(d, axis=-1, keepdims=True)
    dc = d - mu
    dn = dc * lax.rsqrt(jnp.mean(dc * dc, axis=-1, keepdims=True) + EPS)
    dn = dn * lng_ref[...] + lnb_ref[...]
    act = (dn * jax.nn.sigmoid(dn)).astype(BF16)
    y = jnp.dot(act, wpw2_ref[...], preferred_element_type=F32) + bpw2_ref[...]
    x1buf_ref[...] = x_ref[...] + _rms(y, gpost_ref[...])


def _odd_call(x3d, g_pre, g_post, w_pw1, b_pw1, w_dw, b_dw, ln_g, ln_b, w_pw2, b_pw2,
              mg_pre, mg_post, w1, w2, layer, idx):
    bsz, seq, d = x3d.shape
    rows = TILE_ROWS
    assert seq % rows == 0 and rows % SUBLANES == 0 and rows >= CF_HALO
    n_tiles = seq // rows
    hidden = w1.shape[-1]
    x_spec, o_spec = _tile_specs(rows, d, n_tiles)
    hbm = pl.BlockSpec(memory_space=pl.ANY)
    kern = functools.partial(_odd_kernel, rows=rows, d_model=d, layer=layer, idx=idx)
    return pl.pallas_call(
        kern,
        grid=(bsz, n_tiles + 1),
        in_specs=[
            x_spec,
            _const_spec((None, 1, d), lambda b, s: (layer, 0, 0)),
            _const_spec((None, 1, d), lambda b, s: (layer, 0, 0)),
            hbm,
            _const_spec((None, 1, 2 * d), lambda b, s: (idx, 0, 0)),
            _const_spec((None, CF_KERNEL, d), lambda b, s: (idx, 0, 0)),
            _const_spec((None, 1, d), lambda b, s: (idx, 0, 0)),
            _const_spec((None, 1, d), lambda b, s: (idx, 0, 0)),
            _const_spec((None, 1, d), lambda b, s: (idx, 0, 0)),
            hbm,
            _const_spec((None, 1, d), lambda b, s: (idx, 0, 0)),
            _const_spec((None, 1, d), lambda b, s: (layer, 0, 0)),
            _const_spec((None, 1, d), lambda b, s: (layer, 0, 0)),
            hbm,
            hbm,
        ],
        out_specs=o_spec,
        out_shape=jax.ShapeDtypeStruct((bsz, seq, d), F32),
        scratch_shapes=[
            pltpu.VMEM((rows, d), F32),
            pltpu.VMEM((rows + CF_HALO, d), F32),
            pltpu.VMEM((rows, d), F32),
            pltpu.VMEM((d, 2 * d), BF16),
            pltpu.VMEM((d, d), BF16),
            pltpu.VMEM((d, hidden), BF16),
            pltpu.VMEM((hidden, d), BF16),
            pltpu.VMEM((2, _stage_rows(d), d), F32),
            pltpu.VMEM((2, _stage_rows(2 * d), 2 * d), F32),
            pltpu.VMEM((2, _stage_rows(hidden), hidden), F32),
            pltpu.SemaphoreType.DMA((2,)),
        ],
        compiler_params=pltpu.CompilerParams(
            dimension_semantics=("arbitrary", "arbitrary"),
            vmem_limit_bytes=VMEM_LIMIT_BYTES),
        name="conformer_layer",
    )(x3d, g_pre, g_post, w_pw1, b_pw1, w_dw, b_dw, ln_g, ln_b, w_pw2, b_pw2,
      mg_pre, mg_post, w1, w2)


def kernel(x, norm_mix_pre, norm_mix_post, norm_mlp_pre, norm_mlp_post,
           ab_w_in, gla_w_gk2, gla_b_gk2, sc_w_conv, gla_norm_g, ab_w_out,
           cf_w_pw1, cf_b_pw1, cf_w_dw, cf_b_dw, cf_ln_g, cf_ln_b, cf_w_pw2, cf_b_pw2,
           mlp_w1, mlp_w2):
    depth = norm_mix_pre.shape[0]
    n_in = ab_w_in.shape[-1]
    n_main = n_in - GLA_RANK
    assert n_main % LANES == 0 and mlp_w1.shape[-1] % MLP_HIDDEN_CHUNK == 0

    w_gk1 = jnp.pad(ab_w_in[:, :, n_main:].astype(BF16), ((0, 0), (0, 0), (0, LANES - GLA_RANK)))
    w_gk2 = jnp.pad(gla_w_gk2.astype(BF16), ((0, 0), (0, LANES - GLA_RANK), (0, 0)))
    w_in, w_out, w_pw1, w_pw2, w1, w2 = ab_w_in, ab_w_out, cf_w_pw1, cf_w_pw2, mlp_w1, mlp_w2

    def rows3d(v):
        return v.reshape(v.shape[0], 1, v.shape[1])

    norm_mix_pre, norm_mix_post, norm_mlp_pre, norm_mlp_post = map(
        rows3d, (norm_mix_pre, norm_mix_post, norm_mlp_pre, norm_mlp_post))
    gla_b_gk2, gla_norm_g = rows3d(gla_b_gk2), rows3d(gla_norm_g)
    cf_b_pw1, cf_b_dw, cf_ln_g, cf_ln_b, cf_b_pw2 = map(
        rows3d, (cf_b_pw1, cf_b_dw, cf_ln_g, cf_ln_b, cf_b_pw2))

    for layer in range(depth):
        idx = layer // 2
        if layer % 2 == 0:
            x = _even_call(x, norm_mix_pre, norm_mix_post, w_in, w_gk1, w_gk2, gla_b_gk2,
                           sc_w_conv, gla_norm_g, w_out, norm_mlp_pre, norm_mlp_post,
                           w1, w2, layer, idx)
        else:
            x = _odd_call(x, norm_mix_pre, norm_mix_post, w_pw1, cf_b_pw1, cf_w_dw,
                          cf_b_dw, cf_ln_g, cf_ln_b, w_pw2, cf_b_pw2, norm_mlp_pre,
                          norm_mlp_post, w1, w2, layer, idx)
    return x
```

```python
import functools

import jax
import jax.numpy as jnp
from jax import lax
from jax.experimental import pallas as pl
from jax.experimental.pallas import tpu as pltpu

F32 = jnp.float32
BF16 = jnp.bfloat16

EPS = 1e-6
GLA_HEADS = 4
GLA_RANK = 16
GLA_GATE_NORM = 16.0
SC_KERNEL = 3
CF_KERNEL = 31

LANES = 128
SUBLANES = 8
MXU_DIM = 256
VMEM_LIMIT_BYTES = 56 * 1024 * 1024
WEIGHT_STAGE_SLOTS = 4
WEIGHT_STAGE_BYTES = 512 * 1024

TILE_ROWS = 256
MLP_HIDDEN_CHUNK = 1024
GLA_CHUNK = 128
EVEN_HEAD_ORDER = (0, 0, 0)
GLA_FAST_PATH_MAX_DECAY = 60.0
CF_HALO = 32


def _rms(x, g):
    ms = jnp.mean(x * x, axis=-1, keepdims=True)
    return x * lax.rsqrt(ms + EPS) * g


def _const_spec(shape, index_map):
    return pl.BlockSpec(shape, index_map, pipeline_mode=pl.Buffered(1))


def _zero_bits_of(v):
    u = lax.bitcast_convert_type(v, jnp.uint32)
    return lax.shift_right_logical(lax.shift_right_logical(u, jnp.uint32(16)), jnp.uint32(16))


def _order_after(ref, block, token):
    bits = lax.bitcast_convert_type(ref[block], jnp.uint32) | token
    ref[block] = lax.bitcast_convert_type(bits, F32)


def _load_weight_as_bf16(src_hbm_ref, dst_ref, stage_ref, sem_ref):
    k_rows, n_cols = dst_ref.shape
    n_slots, chunk = stage_ref.shape[0], stage_ref.shape[1]
    assert k_rows % chunk == 0 and stage_ref.shape[2] == n_cols
    n_chunks = k_rows // chunk
    ahead = n_slots - 1

    def copy(i):
        return pltpu.make_async_copy(src_hbm_ref.at[pl.ds(i * chunk, chunk), :],
                                     stage_ref.at[i % n_slots], sem_ref.at[i % n_slots])

    for i in range(min(ahead, n_chunks)):
        copy(i).start()
    for i in range(n_chunks):
        copy(i).wait()
        dst_ref[i * chunk:(i + 1) * chunk, :] = stage_ref[i % n_slots].astype(BF16)
        if i + ahead < n_chunks:
            copy(i + ahead).start()


def _stage_rows(n_cols):
    return WEIGHT_STAGE_BYTES // (4 * n_cols)


def _stage_scratch(n_cols):
    return pltpu.VMEM((WEIGHT_STAGE_SLOTS, _stage_rows(n_cols), n_cols), F32)


def _store_mixer_output(x1, mgpre_ref, x1buf_ref, hbuf_ref):
    x1buf_ref[...] = x1
    hbuf_ref[...] = _rms(x1, mgpre_ref[...]).astype(BF16)


def _mlp_stage(x1buf_ref, h, gpost_ref, w1_ref, w2_ref, o_ref, done, *, tail_parts):
    rows = h.shape[0]
    part = rows // tail_parts
    hidden = w1_ref.shape[1]
    n_chunks = hidden // MLP_HIDDEN_CHUNK

    def cols(c):
        return slice(c * MLP_HIDDEN_CHUNK, (c + 1) * MLP_HIDDEN_CHUNK)

    def up(c):
        raw = jnp.dot(h, w1_ref[:, cols(c)], preferred_element_type=F32)
        raw = jnp.maximum(raw, 0.0)
        return (raw * raw).astype(BF16)

    acc = None
    act = up(0)
    yield
    for c in range(1, n_chunks):
        act_next = up(c)
        yield
        p = jnp.dot(act, w2_ref[cols(c - 1), :], preferred_element_type=F32)
        acc = p if acc is None else acc + p
        act = act_next
        yield
    out = None
    for t in range(tail_parts):
        sl = slice(t * part, (t + 1) * part)
        p = jnp.dot(act[sl], w2_ref[cols(n_chunks - 1), :], preferred_element_type=F32)
        total = p if acc is None else acc[sl] + p
        out = x1buf_ref[sl, :] + _rms(total, gpost_ref[...])
        o_ref[sl, :] = out
    done.append(_zero_bits_of(out[0:SUBLANES, 0:LANES]))


def _interleave(*gens, head=()):
    live = list(gens)

    def step(g):
        try:
            next(g)
        except StopIteration:
            live.remove(g)

    for i in head:
        step(gens[i])
    while live:
        for g in list(live):
            step(g)


def _even_mixer_stage(x_ref, gpre_ref, win_ref, wgk2_ref, bgk2_ref, wsc_ref, gng_ref, wout_ref,
                      z_ref, mix_ref, ubuf_ref, s_ref, bbuf_ref, sprev_ref, result,
                      *, rows, chunk, d_model):
    sc = d_model // 2
    dk = d_model // 2 // GLA_HEADS
    dv = d_model // GLA_HEADS
    key = GLA_HEADS * dk
    o_scx, o_scb, o_scc = 0, sc, 2 * sc
    o_q = 3 * sc
    o_k = o_q + key
    o_v = o_k + key
    o_g = o_v + GLA_HEADS * dv
    o_lr = o_g + GLA_HEADS * dv

    h = _rms(x_ref[...], gpre_ref[...]).astype(BF16)
    z_ref[...] = jnp.dot(h, win_ref[...], preferred_element_type=F32)
    yield

    gk_lr = z_ref[:, o_lr:o_lr + LANES].astype(BF16)
    gk = jnp.dot(gk_lr, wgk2_ref[...], preferred_element_type=F32) + bgk2_ref[...]
    yield
    g = (jnp.minimum(gk, 0.0) - jnp.log1p(jnp.exp(-jnp.abs(gk)))) * (1.0 / GLA_GATE_NORM)
    g_hi = g.astype(BF16)
    g_lo = (g - g_hi.astype(F32)).astype(BF16)
    ri = lax.broadcasted_iota(jnp.int32, (rows, rows), 0)
    ci = lax.broadcasted_iota(jnp.int32, (rows, rows), 1)
    same_chunk = (ri // chunk) == (ci // chunk)
    tril_chunks = jnp.where(same_chunk & (ci <= ri), 1.0, 0.0).astype(BF16)
    b = (jnp.dot(tril_chunks, g_hi, preferred_element_type=F32)
         + jnp.dot(tril_chunks, g_lo, preferred_element_type=F32))
    bbuf_ref[...] = b
    sprev_ref[...] = s_ref[...]
    n_chunks = rows // chunk
    chunk_totals = jnp.concatenate(
        [b[(c + 1) * chunk - 1:(c + 1) * chunk, :] for c in range(n_chunks)], axis=0)
    result.append(jnp.min(chunk_totals))
    yield

    u = z_ref[:, o_scc:o_scc + sc] * z_ref[:, o_scx:o_scx + sc]
    ubuf_ref[SUBLANES:SUBLANES + rows, :] = u
    conv = wsc_ref[SC_KERNEL - 1:SC_KERNEL, :] * u
    for j in range(SC_KERNEL - 1):
        shift = SC_KERNEL - 1 - j
        conv = conv + wsc_ref[j:j + 1, :] * ubuf_ref[SUBLANES - shift:SUBLANES - shift + rows, :]
    ubuf_ref[0:SUBLANES, :] = u[rows - SUBLANES:rows, :]
    mix_ref[:, 0:sc] = (z_ref[:, o_scb:o_scb + sc] * conv).astype(BF16)

    causal = (lax.broadcasted_iota(jnp.int32, (chunk, chunk), 1)
              <= lax.broadcasted_iota(jnp.int32, (chunk, chunk), 0))
    scale = dk ** -0.5
    nt_dims = (((1,), (1,)), ((), ()))
    tn_dims = (((0,), (0,)), ((), ()))
    gng = gng_ref[...]
    pairs = [(hd, c) for hd in range(GLA_HEADS) for c in range(n_chunks)]

    q_in, v_bf, decay, attn, kv = {}, {}, {}, {}, {}
    for hd, c in pairs:
        r0 = c * chunk
        bc = b[r0:r0 + chunk, hd * dk:(hd + 1) * dk]
        qc = z_ref[r0:r0 + chunk, o_q + hd * dk:o_q + (hd + 1) * dk]
        kc = z_ref[r0:r0 + chunk, o_k + hd * dk:o_k + (hd + 1) * dk]
        v_bf[hd, c] = z_ref[r0:r0 + chunk, o_v + hd * dv:o_v + (hd + 1) * dv].astype(BF16)
        b_last = bc[chunk - 1:chunk, :]
        b_mid = bc[chunk // 2 - 1:chunk // 2, :]
        qd = qc * jnp.exp(bc) * scale
        q_mid = (qd * jnp.exp(-b_mid)).astype(BF16)
        k_mid = kc * jnp.exp(b_mid - bc)
        k_end = (kc * jnp.exp(b_last - bc)).astype(BF16)
        q_in[hd, c] = qd.astype(BF16)
        decay[hd, c] = jnp.broadcast_to(jnp.exp(b_last), (dk, dk)).T
        attn[hd, c] = lax.dot_general(q_mid, k_mid.astype(BF16), nt_dims,
                                      preferred_element_type=F32)
        kv[hd, c] = lax.dot_general(k_end, v_bf[hd, c], tn_dims, preferred_element_type=F32)
    yield

    outs = {}
    for hd in range(GLA_HEADS):
        state = s_ref[hd]
        for c in range(n_chunks):
            a = jnp.where(causal, attn[hd, c], 0.0).astype(BF16)
            outs[hd, c] = (jnp.dot(q_in[hd, c], state.astype(BF16), preferred_element_type=F32)
                           + jnp.dot(a, v_bf[hd, c], preferred_element_type=F32))
            state = (state * jnp.concatenate([decay[hd, c]] * (dv // dk), axis=1) + kv[hd, c])
        s_ref[hd] = state
    yield

    for hd, c in pairs:
        r0 = c * chunk
        o = outs[hd, c]
        gout = z_ref[r0:r0 + chunk, o_g + hd * dv:o_g + (hd + 1) * dv]
        on = o * lax.rsqrt(jnp.mean(o * o, axis=-1, keepdims=True) + EPS) * gng
        yb = on * (gout * jax.nn.sigmoid(gout))
        mix_ref[r0:r0 + chunk, sc + hd * dv:sc + (hd + 1) * dv] = yb.astype(BF16)
    result.append(jnp.dot(mix_ref[...], wout_ref[...], preferred_element_type=F32))


def _even_exact_tile(x_ref, gpost_ref, gng_ref, wout_ref, mgpre_ref, x1buf_ref, hbuf_ref,
                     z_ref, mix_ref, bbuf_ref, sprev_ref, orow_ref, *, rows, chunk, d_model):
    sc = d_model // 2
    dk = d_model // 2 // GLA_HEADS
    dv = d_model // GLA_HEADS
    key = GLA_HEADS * dk
    o_q = 3 * sc
    o_k = o_q + key
    o_v = o_k + key
    o_g = o_v + GLA_HEADS * dv
    scale = dk ** -0.5
    tn_dims = (((0,), (0,)), ((), ()))
    gng = gng_ref[...]
    row_id = lax.broadcasted_iota(jnp.int32, (chunk, 1), 0)
    for hd in range(GLA_HEADS):
        klanes = slice(hd * dk, (hd + 1) * dk)
        qlanes = slice(o_q + hd * dk, o_q + (hd + 1) * dk)
        state = sprev_ref[hd]
        for c in range(rows // chunk):
            r0 = c * chunk
            bc = bbuf_ref[r0:r0 + chunk, klanes]
            kc = z_ref[r0:r0 + chunk, o_k + hd * dk:o_k + (hd + 1) * dk]
            vc = z_ref[r0:r0 + chunk, o_v + hd * dv:o_v + (hd + 1) * dv]
            gout = z_ref[r0:r0 + chunk, o_g + hd * dv:o_g + (hd + 1) * dv]
            b_last = bc[chunk - 1:chunk, :]
            q_in = (z_ref[r0:r0 + chunk, qlanes] * jnp.exp(bc) * scale).astype(BF16)
            k_end = (kc * jnp.exp(b_last - bc)).astype(BF16)

            def eight_rows(blk, carry):
                base = pl.multiple_of(blk * SUBLANES, SUBLANES)
                b_blk = bbuf_ref[pl.ds(r0 + base, SUBLANES), klanes]
                q_blk = z_ref[pl.ds(r0 + base, SUBLANES), qlanes] * scale
                out_rows = []
                for r in range(SUBLANES):
                    decay_ij = jnp.exp(jnp.minimum(b_blk[r:r + 1, :] - bc, 0.0))
                    w = (q_blk[r:r + 1, :] * decay_ij) * kc
                    a = jnp.where(row_id <= base + r, jnp.sum(w, axis=-1, keepdims=True), 0.0)
                    out_rows.append(jnp.sum(a * vc, axis=0, keepdims=True))
                orow_ref[pl.ds(base, SUBLANES), :] = jnp.concatenate(out_rows, axis=0)
                return carry

            lax.fori_loop(0, chunk // SUBLANES, eight_rows, 0)
            o = (jnp.dot(q_in, state.astype(BF16), preferred_element_type=F32) + orow_ref[...])
            decay = jnp.broadcast_to(jnp.exp(b_last), (dk, dk)).T
            state = (state * jnp.concatenate([decay] * (dv // dk), axis=1)
                     + lax.dot_general(k_end, vc.astype(BF16), tn_dims,
                                       preferred_element_type=F32))
            on = o * lax.rsqrt(jnp.mean(o * o, axis=-1, keepdims=True) + EPS) * gng
            yb = on * (gout * jax.nn.sigmoid(gout))
            mix_ref[r0:r0 + chunk, sc + hd * dv:sc + (hd + 1) * dv] = yb.astype(BF16)
    y = jnp.dot(mix_ref[...], wout_ref[...], preferred_element_type=F32)
    _store_mixer_output(x_ref[...] + _rms(y, gpost_ref[...]), mgpre_ref, x1buf_ref, hbuf_ref)


def _even_kernel(x_ref, gpre_ref, gpost_ref, win_ref, wgk2_ref, bgk2_ref, wsc_ref, gng_ref,
                 wout_hbm, mgpre_ref, mgpost_ref, w1_hbm, w2_hbm, o_ref,
                 x1buf_ref, hbuf_ref, z_ref, mix_ref, ubuf_ref, s_ref, bbuf_ref, sprev_ref,
                 orow_ref, wout_ref, w1_ref, w2_ref, stage_d_ref, stage_h_ref, sem_ref,
                 *, rows, chunk, d_model, layer, idx):
    @pl.when(pl.program_id(1) == 0)
    def _():
        x1buf_ref[...] = jnp.zeros_like(x1buf_ref)
        hbuf_ref[...] = jnp.zeros_like(hbuf_ref)
        s_ref[...] = jnp.zeros_like(s_ref)
        ubuf_ref[0:SUBLANES, :] = jnp.zeros((SUBLANES, ubuf_ref.shape[1]), F32)
        _load_weight_as_bf16(w1_hbm.at[layer], w1_ref, stage_h_ref, sem_ref)
        _load_weight_as_bf16(w2_hbm.at[layer], w2_ref, stage_d_ref, sem_ref)
        _load_weight_as_bf16(wout_hbm.at[idx], wout_ref, stage_d_ref, sem_ref)

    mixed, mlp_done = [], []
    _interleave(
        _mlp_stage(x1buf_ref, hbuf_ref[...], mgpost_ref, w1_ref, w2_ref, o_ref, mlp_done,
                   tail_parts=2),
        _even_mixer_stage(x_ref, gpre_ref, win_ref, wgk2_ref, bgk2_ref, wsc_ref, gng_ref,
                          wout_ref, z_ref, mix_ref, ubuf_ref, s_ref, bbuf_ref, sprev_ref, mixed,
                          rows=rows, chunk=chunk, d_model=d_model),
        head=EVEN_HEAD_ORDER)
    min_chunk_decay, y = mixed
    _store_mixer_output(x_ref[...] + _rms(y, gpost_ref[...]), mgpre_ref, x1buf_ref, hbuf_ref)

    @pl.when(min_chunk_decay < -GLA_FAST_PATH_MAX_DECAY)
    def _():
        _even_exact_tile(x_ref, gpost_ref, gng_ref, wout_ref, mgpre_ref, x1buf_ref, hbuf_ref,
                         z_ref, mix_ref, bbuf_ref, sprev_ref, orow_ref,
                         rows=rows, chunk=chunk, d_model=d_model)


def _tile_specs(rows, d, n_tiles):
    x_spec = pl.BlockSpec((None, rows, d), lambda b, s: (b, jnp.minimum(s, n_tiles - 1), 0))
    o_spec = pl.BlockSpec((None, rows, d), lambda b, s: (b, jnp.maximum(s - 1, 0), 0))
    return x_spec, o_spec


def _even_call(x3d, g_pre, g_post, w_in, w_gk2, b_gk2, w_sc, gn_g, w_out,
               mg_pre, mg_post, w1, w2, layer, idx):
    bsz, seq, d = x3d.shape
    rows, chunk = TILE_ROWS, GLA_CHUNK
    assert seq % rows == 0 and rows % chunk == 0
    n_tiles = seq // rows
    n_in = w_in.shape[-1]
    key = w_gk2.shape[-1]
    sc = w_sc.shape[-1]
    mix = w_out.shape[1]
    hidden = w1.shape[-1]
    dk = key // GLA_HEADS
    dv = d // GLA_HEADS
    x_spec, o_spec = _tile_specs(rows, d, n_tiles)
    hbm = pl.BlockSpec(memory_space=pl.ANY)
    kern = functools.partial(_even_kernel, rows=rows, chunk=chunk, d_model=d,
                             layer=layer, idx=idx)
    return pl.pallas_call(
        kern,
        grid=(bsz, n_tiles + 1),
        in_specs=[
            x_spec,
            _const_spec((None, 1, d), lambda b, s: (layer, 0, 0)),
            _const_spec((None, 1, d), lambda b, s: (layer, 0, 0)),
            _const_spec((None, d, n_in), lambda b, s: (idx, 0, 0)),
            _const_spec((None, LANES, key), lambda b, s: (idx, 0, 0)),
            _const_spec((None, 1, key), lambda b, s: (idx, 0, 0)),
            _const_spec((None, SC_KERNEL, sc), lambda b, s: (idx, 0, 0)),
            _const_spec((None, 1, dv), lambda b, s: (idx, 0, 0)),
            hbm,
            _const_spec((None, 1, d), lambda b, s: (layer, 0, 0)),
            _const_spec((None, 1, d), lambda b, s: (layer, 0, 0)),
            hbm,
            hbm,
        ],
        out_specs=o_spec,
        out_shape=jax.ShapeDtypeStruct((bsz, seq, d), F32),
        scratch_shapes=[
            pltpu.VMEM((rows, d), F32),
            pltpu.VMEM((rows, d), BF16),
            pltpu.VMEM((rows, n_in), F32),
            pltpu.VMEM((rows, mix), BF16),
            pltpu.VMEM((rows + SUBLANES, sc), F32),
            pltpu.VMEM((GLA_HEADS, dk, dv), F32),
            pltpu.VMEM((rows, key), F32),
            pltpu.VMEM((GLA_HEADS, dk, dv), F32),
            pltpu.VMEM((chunk, dv), F32),
            pltpu.VMEM((mix, d), BF16),
            pltpu.VMEM((d, hidden), BF16),
            pltpu.VMEM((hidden, d), BF16),
            _stage_scratch(d),
            _stage_scratch(hidden),
            pltpu.SemaphoreType.DMA((WEIGHT_STAGE_SLOTS,)),
        ],
        compiler_params=pltpu.CompilerParams(
            dimension_semantics=("arbitrary", "arbitrary"),
            vmem_limit_bytes=VMEM_LIMIT_BYTES),
        name="shortconv_gla_layer",
    )(x3d, g_pre, g_post, w_in, w_gk2, b_gk2, w_sc, gn_g, w_out, mg_pre, mg_post, w1, w2)


def _depthwise_conv(ubuf_ref, wdw_ref, bdw_ref, dbuf_ref, *, rows, d_model):
    sub = SUBLANES
    halo_blocks = CF_HALO // sub
    n_blocks = rows // sub
    max_a = (CF_KERNEL - 1) // sub
    assert max_a < halo_blocks
    row_id = lax.broadcasted_iota(jnp.int32, (sub, LANES), 0)
    for lg in range(d_model // LANES):
        lanes = slice(lg * LANES, (lg + 1) * LANES)
        w = {}
        for s in range(CF_KERNEL):
            w[s] = jnp.broadcast_to(wdw_ref[CF_KERNEL - 1 - s:CF_KERNEL - s, lanes], (sub, LANES))
        bias = jnp.broadcast_to(bdw_ref[:, lanes], (sub, LANES))
        blocks = {}

        def block(m):
            if m not in blocks:
                blocks[m] = ubuf_ref[m * sub:(m + 1) * sub, lanes]
            return blocks[m]

        prev_q = None
        for m in range(halo_blocks - 1, halo_blocks + n_blocks):
            q = []
            for r in range(sub):
                acc = None
                for a in range(max_a + 1):
                    s = sub * a + r
                    if s >= CF_KERNEL:
                        continue
                    term = w[s] * block(m - a)
                    acc = term if acc is None else acc + term
                q.append(acc)
            if m >= halo_blocks:
                out = bias + q[0]
                for r in range(1, sub):
                    both = jnp.where(row_id < sub - r, q[r], prev_q[r])
                    out = out + pltpu.roll(both, r, axis=0)
                k = m - halo_blocks
                dbuf_ref[k * sub:(k + 1) * sub, lanes] = out
            prev_q = q
            blocks.pop(m - max_a, None)
        yield


def _odd_kernel(x_ref, gpre_ref, gpost_ref, wpw1_hbm, bpw1_ref, wdw_ref, bdw_ref,
                lng_ref, lnb_ref, wpw2_hbm, bpw2_ref, mgpre_ref, mgpost_ref, w1_hbm, w2_hbm,
                o_ref, x1buf_ref, ubuf_ref, dbuf_ref, wpw1_ref, wpw2_ref, w1_ref, w2_ref,
                stage_d_ref, stage_2d_ref, stage_h_ref, sem_ref, *, rows, d_model, layer, idx):
    @pl.when(pl.program_id(1) == 0)
    def _():
        x1buf_ref[...] = jnp.zeros_like(x1buf_ref)
        ubuf_ref[0:CF_HALO, :] = jnp.zeros((CF_HALO, d_model), F32)
        _load_weight_as_bf16(wpw1_hbm.at[idx], wpw1_ref, stage_2d_ref, sem_ref)
        _load_weight_as_bf16(w1_hbm.at[layer], w1_ref, stage_h_ref, sem_ref)
        _load_weight_as_bf16(w2_hbm.at[layer], w2_ref, stage_d_ref, sem_ref)
        _load_weight_as_bf16(wpw2_hbm.at[idx], wpw2_ref, stage_d_ref, sem_ref)

    h = _rms(x_ref[...], gpre_ref[...]).astype(BF16)
    u = jnp.dot(h, wpw1_ref[...], preferred_element_type=F32) + bpw1_ref[...]
    ubuf_ref[CF_HALO:CF_HALO + rows, :] = u[:, :d_model] * jax.nn.sigmoid(u[:, d_model:])

    mlp_done = []
    h_mlp = _rms(x1buf_ref[...], mgpre_ref[...]).astype(BF16)
    _interleave(
        _mlp_stage(x1buf_ref, h_mlp, mgpost_ref, w1_ref, w2_ref, o_ref, mlp_done, tail_parts=1),
        _depthwise_conv(ubuf_ref, wdw_ref, bdw_ref, dbuf_ref, rows=rows, d_model=d_model))
    ubuf_ref[0:CF_HALO, :] = ubuf_ref[rows:rows + CF_HALO, :]

    _order_after(dbuf_ref, (slice(0, SUBLANES), slice(0, LANES)), mlp_done[-1])
    d = dbuf_ref[...]
    mu = jnp.mean(d, axis=-1, keepdims=True)
    dc = d - mu
    dn = dc * lax.rsqrt(jnp.mean(dc * dc, axis=-1, keepdims=True) + EPS)
    dn = dn * lng_ref[...] + lnb_ref[...]
    act = (dn * jax.nn.sigmoid(dn)).astype(BF16)
    y = jnp.dot(act, wpw2_ref[...], preferred_element_type=F32) + bpw2_ref[...]
    x1buf_ref[...] = x_ref[...] + _rms(y, gpost_ref[...])


def _odd_call(x3d, g_pre, g_post, w_pw1, b_pw1, w_dw, b_dw, ln_g, ln_b, w_pw2, b_pw2,
              mg_pre, mg_post, w1, w2, layer, idx):
    bsz, seq, d = x3d.shape
    rows = TILE_ROWS
    assert seq % rows == 0 and rows % SUBLANES == 0 and rows >= CF_HALO
    n_tiles = seq // rows
    hidden = w1.shape[-1]
    x_spec, o_spec = _tile_specs(rows, d, n_tiles)
    hbm = pl.BlockSpec(memory_space=pl.ANY)
    kern = functools.partial(_odd_kernel, rows=rows, d_model=d, layer=layer, idx=idx)
    return pl.pallas_call(
        kern,
        grid=(bsz, n_tiles + 1),
        in_specs=[
            x_spec,
            _const_spec((None, 1, d), lambda b, s: (layer, 0, 0)),
            _const_spec((None, 1, d), lambda b, s: (layer, 0, 0)),
            hbm,
            _const_spec((None, 1, 2 * d), lambda b, s: (idx, 0, 0)),
            _const_spec((None, CF_KERNEL, d), lambda b, s: (idx, 0, 0)),
            _const_spec((None, 1, d), lambda b, s: (idx, 0, 0)),
            _const_spec((None, 1, d), lambda b, s: (idx, 0, 0)),
            _const_spec((None, 1, d), lambda b, s: (idx, 0, 0)),
            hbm,
            _const_spec((None, 1, d), lambda b, s: (idx, 0, 0)),
            _const_spec((None, 1, d), lambda b, s: (layer, 0, 0)),
            _const_spec((None, 1, d), lambda b, s: (layer, 0, 0)),
            hbm,
            hbm,
        ],
        out_specs=o_spec,
        out_shape=jax.ShapeDtypeStruct((bsz, seq, d), F32),
        scratch_shapes=[
            pltpu.VMEM((rows, d), F32),
            pltpu.VMEM((rows + CF_HALO, d), F32),
            pltpu.VMEM((rows, d), F32),
            pltpu.VMEM((d, 2 * d), BF16),
            pltpu.VMEM((d, d), BF16),
            pltpu.VMEM((d, hidden), BF16),
            pltpu.VMEM((hidden, d), BF16),
            _stage_scratch(d),
            _stage_scratch(2 * d),
            _stage_scratch(hidden),
            pltpu.SemaphoreType.DMA((WEIGHT_STAGE_SLOTS,)),
        ],
        compiler_params=pltpu.CompilerParams(
            dimension_semantics=("arbitrary", "arbitrary"),
            vmem_limit_bytes=VMEM_LIMIT_BYTES),
        name="conformer_layer",
    )(x3d, g_pre, g_post, w_pw1, b_pw1, w_dw, b_dw, ln_g, ln_b, w_pw2, b_pw2,
      mg_pre, mg_post, w1, w2)


def kernel(x, norm_mix_pre, norm_mix_post, norm_mlp_pre, norm_mlp_post,
           ab_w_in, gla_w_gk2, gla_b_gk2, sc_w_conv, gla_norm_g, ab_w_out,
           cf_w_pw1, cf_b_pw1, cf_w_dw, cf_b_dw, cf_ln_g, cf_ln_b, cf_w_pw2, cf_b_pw2,
           mlp_w1, mlp_w2):
    depth = norm_mix_pre.shape[0]
    n_in = ab_w_in.shape[-1]
    n_main = n_in - GLA_RANK
    assert n_main % LANES == 0 and mlp_w1.shape[-1] % MLP_HIDDEN_CHUNK == 0

    n_pad = -(-(n_main + LANES) // MXU_DIM) * MXU_DIM
    w_in = jnp.pad(ab_w_in.astype(BF16), ((0, 0), (0, 0), (0, n_pad - n_in)))
    w_gk2 = jnp.pad(gla_w_gk2.astype(BF16), ((0, 0), (0, LANES - GLA_RANK), (0, 0)))
    w_out, w_pw1, w_pw2, w1, w2 = ab_w_out, cf_w_pw1, cf_w_pw2, mlp_w1, mlp_w2

    def rows3d(v):
        return v.reshape(v.shape[0], 1, v.shape[1])

    norm_mix_pre, norm_mix_post, norm_mlp_pre, norm_mlp_post = map(
        rows3d, (norm_mix_pre, norm_mix_post, norm_mlp_pre, norm_mlp_post))
    gla_b_gk2, gla_norm_g = rows3d(gla_b_gk2), rows3d(gla_norm_g)
    cf_b_pw1, cf_b_dw, cf_ln_g, cf_ln_b, cf_b_pw2 = map(
        rows3d, (cf_b_pw1, cf_b_dw, cf_ln_g, cf_ln_b, cf_b_pw2))

    for layer in range(depth):
        idx = layer // 2
        if layer % 2 == 0:
            x = _even_call(x, norm_mix_pre, norm_mix_post, w_in, w_gk2, gla_b_gk2,
                           sc_w_conv, gla_norm_g, w_out, norm_mlp_pre, norm_mlp_post,
                           w1, w2, layer, idx)
        else:
            x = _odd_call(x, norm_mix_pre, norm_mix_post, w_pw1, cf_b_pw1, cf_w_dw,
                          cf_b_dw, cf_ln_g, cf_ln_b, w_pw2, cf_b_pw2, norm_mlp_pre,
                          norm_mlp_post, w1, w2, layer, idx)
    return x
```

```python
import functools

import jax
import jax.numpy as jnp
from jax import lax
from jax.experimental import pallas as pl
from jax.experimental.pallas import tpu as pltpu

F32 = jnp.float32
BF16 = jnp.bfloat16

EPS = 1e-6
GLA_HEADS = 4
GLA_RANK = 16
GLA_GATE_NORM = 16.0
SC_KERNEL = 3
CF_KERNEL = 31

LANES = 128
SUBLANES = 8
MXU_DIM = 256
VMEM_LIMIT_BYTES = 56 * 1024 * 1024
WEIGHT_STAGE_SLOTS = 2
WEIGHT_STAGE_BYTES = 2 * 1024 * 1024

TILE_ROWS = 256
MLP_HIDDEN_CHUNK = 1024
GLA_CHUNK = 128
EVEN_HEAD_ORDER = (0, 0, 0)
GLA_FAST_PATH_MAX_DECAY = 60.0
CF_HALO = 32


def _rms(x, g):
    ms = jnp.mean(x * x, axis=-1, keepdims=True)
    return x * lax.rsqrt(ms + EPS) * g


def _const_spec(shape, index_map):
    return pl.BlockSpec(shape, index_map, pipeline_mode=pl.Buffered(1))


def _zero_bits_of(v):
    u = lax.bitcast_convert_type(v, jnp.uint32)
    return lax.shift_right_logical(lax.shift_right_logical(u, jnp.uint32(16)), jnp.uint32(16))


def _order_after(ref, block, token):
    bits = lax.bitcast_convert_type(ref[block], jnp.uint32) | token
    ref[block] = lax.bitcast_convert_type(bits, F32)


def _load_weight_as_bf16(src_hbm_ref, dst_ref, stage_ref, sem_ref):
    k_rows, n_cols = dst_ref.shape
    n_slots, chunk = stage_ref.shape[0], stage_ref.shape[1]
    assert k_rows % chunk == 0 and stage_ref.shape[2] == n_cols
    n_chunks = k_rows // chunk
    ahead = n_slots - 1

    def copy(i):
        return pltpu.make_async_copy(src_hbm_ref.at[pl.ds(i * chunk, chunk), :],
                                     stage_ref.at[i % n_slots], sem_ref.at[i % n_slots])

    for i in range(min(ahead, n_chunks)):
        copy(i).start()
    for i in range(n_chunks):
        copy(i).wait()
        dst_ref[i * chunk:(i + 1) * chunk, :] = stage_ref[i % n_slots].astype(BF16)
        if i + ahead < n_chunks:
            copy(i + ahead).start()


def _stage_rows(n_cols):
    return WEIGHT_STAGE_BYTES // (4 * n_cols)


def _stage_scratch(n_cols):
    return pltpu.VMEM((WEIGHT_STAGE_SLOTS, _stage_rows(n_cols), n_cols), F32)


def _store_mixer_output(x1, mgpre_ref, x1buf_ref, hbuf_ref):
    x1buf_ref[...] = x1
    hbuf_ref[...] = _rms(x1, mgpre_ref[...]).astype(BF16)


def _mlp_stage(x1buf_ref, h, gpost_ref, w1_ref, w2_ref, o_ref, done, *, tail_parts):
    rows = h.shape[0]
    part = rows // tail_parts
    hidden = w1_ref.shape[1]
    n_chunks = hidden // MLP_HIDDEN_CHUNK

    def cols(c):
        return slice(c * MLP_HIDDEN_CHUNK, (c + 1) * MLP_HIDDEN_CHUNK)

    def up(c):
        raw = jnp.dot(h, w1_ref[:, cols(c)], preferred_element_type=F32)
        raw = jnp.maximum(raw, 0.0)
        return (raw * raw).astype(BF16)

    acc = None
    act = up(0)
    yield
    for c in range(1, n_chunks):
        act_next = up(c)
        yield
        p = jnp.dot(act, w2_ref[cols(c - 1), :], preferred_element_type=F32)
        acc = p if acc is None else acc + p
        act = act_next
        yield
    out = None
    for t in range(tail_parts):
        sl = slice(t * part, (t + 1) * part)
        p = jnp.dot(act[sl], w2_ref[cols(n_chunks - 1), :], preferred_element_type=F32)
        total = p if acc is None else acc[sl] + p
        out = x1buf_ref[sl, :] + _rms(total, gpost_ref[...])
        o_ref[sl, :] = out
    done.append(_zero_bits_of(out[0:SUBLANES, 0:LANES]))


def _interleave(*gens, head=()):
    live = list(gens)

    def step(g):
        try:
            next(g)
        except StopIteration:
            live.remove(g)

    for i in head:
        step(gens[i])
    while live:
        for g in list(live):
            step(g)


def _even_mixer_stage(x_ref, gpre_ref, win_ref, wgk2_ref, bgk2_ref, wsc_ref, gng_ref, wout_ref,
                      z_ref, mix_ref, ubuf_ref, s_ref, bbuf_ref, sprev_ref, result,
                      *, rows, chunk, d_model):
    sc = d_model // 2
    dk = d_model // 2 // GLA_HEADS
    dv = d_model // GLA_HEADS
    key = GLA_HEADS * dk
    o_scx, o_scb, o_scc = 0, sc, 2 * sc
    o_q = 3 * sc
    o_k = o_q + key
    o_v = o_k + key
    o_g = o_v + GLA_HEADS * dv
    o_lr = o_g + GLA_HEADS * dv

    h = _rms(x_ref[...], gpre_ref[...]).astype(BF16)
    z_ref[...] = jnp.dot(h, win_ref[...], preferred_element_type=F32)
    yield

    gk_lr = z_ref[:, o_lr:o_lr + LANES].astype(BF16)
    gk = jnp.dot(gk_lr, wgk2_ref[...], preferred_element_type=F32) + bgk2_ref[...]
    yield
    g = (jnp.minimum(gk, 0.0) - jnp.log1p(jnp.exp(-jnp.abs(gk)))) * (1.0 / GLA_GATE_NORM)
    g_hi = g.astype(BF16)
    g_lo = (g - g_hi.astype(F32)).astype(BF16)
    ri = lax.broadcasted_iota(jnp.int32, (rows, rows), 0)
    ci = lax.broadcasted_iota(jnp.int32, (rows, rows), 1)
    same_chunk = (ri // chunk) == (ci // chunk)
    tril_chunks = jnp.where(same_chunk & (ci <= ri), 1.0, 0.0).astype(BF16)
    b = (jnp.dot(tril_chunks, g_hi, preferred_element_type=F32)
         + jnp.dot(tril_chunks, g_lo, preferred_element_type=F32))
    bbuf_ref[...] = b
    sprev_ref[...] = s_ref[...]
    n_chunks = rows // chunk
    chunk_totals = jnp.concatenate(
        [b[(c + 1) * chunk - 1:(c + 1) * chunk, :] for c in range(n_chunks)], axis=0)
    result.append(jnp.min(chunk_totals))
    yield

    u = z_ref[:, o_scc:o_scc + sc] * z_ref[:, o_scx:o_scx + sc]
    ubuf_ref[SUBLANES:SUBLANES + rows, :] = u
    conv = wsc_ref[SC_KERNEL - 1:SC_KERNEL, :] * u
    for j in range(SC_KERNEL - 1):
        shift = SC_KERNEL - 1 - j
        conv = conv + wsc_ref[j:j + 1, :] * ubuf_ref[SUBLANES - shift:SUBLANES - shift + rows, :]
    ubuf_ref[0:SUBLANES, :] = u[rows - SUBLANES:rows, :]
    mix_ref[:, 0:sc] = (z_ref[:, o_scb:o_scb + sc] * conv).astype(BF16)

    causal = (lax.broadcasted_iota(jnp.int32, (chunk, chunk), 1)
              <= lax.broadcasted_iota(jnp.int32, (chunk, chunk), 0))
    scale = dk ** -0.5
    nt_dims = (((1,), (1,)), ((), ()))
    tn_dims = (((0,), (0,)), ((), ()))
    gng = gng_ref[...]
    pairs = [(hd, c) for hd in range(GLA_HEADS) for c in range(n_chunks)]

    q_in, v_bf, decay, attn, kv = {}, {}, {}, {}, {}
    for hd, c in pairs:
        r0 = c * chunk
        bc = b[r0:r0 + chunk, hd * dk:(hd + 1) * dk]
        qc = z_ref[r0:r0 + chunk, o_q + hd * dk:o_q + (hd + 1) * dk]
        kc = z_ref[r0:r0 + chunk, o_k + hd * dk:o_k + (hd + 1) * dk]
        v_bf[hd, c] = z_ref[r0:r0 + chunk, o_v + hd * dv:o_v + (hd + 1) * dv].astype(BF16)
        b_last = bc[chunk - 1:chunk, :]
        b_mid = bc[chunk // 2 - 1:chunk // 2, :]
        qd = qc * jnp.exp(bc) * scale
        q_mid = (qd * jnp.exp(-b_mid)).astype(BF16)
        k_mid = kc * jnp.exp(b_mid - bc)
        k_end = (kc * jnp.exp(b_last - bc)).astype(BF16)
        q_in[hd, c] = qd.astype(BF16)
        decay[hd, c] = jnp.broadcast_to(jnp.exp(b_last), (dk, dk)).T
        attn[hd, c] = lax.dot_general(q_mid, k_mid.astype(BF16), nt_dims,
                                      preferred_element_type=F32)
        kv[hd, c] = lax.dot_general(k_end, v_bf[hd, c], tn_dims, preferred_element_type=F32)
    yield

    outs = {}
    for hd in range(GLA_HEADS):
        state = s_ref[hd]
        for c in range(n_chunks):
            a = jnp.where(causal, attn[hd, c], 0.0).astype(BF16)
            outs[hd, c] = (jnp.dot(q_in[hd, c], state.astype(BF16), preferred_element_type=F32)
                           + jnp.dot(a, v_bf[hd, c], preferred_element_type=F32))
            state = (state * jnp.concatenate([decay[hd, c]] * (dv // dk), axis=1) + kv[hd, c])
        s_ref[hd] = state
    yield

    for hd, c in pairs:
        r0 = c * chunk
        o = outs[hd, c]
        gout = z_ref[r0:r0 + chunk, o_g + hd * dv:o_g + (hd + 1) * dv]
        on = o * lax.rsqrt(jnp.mean(o * o, axis=-1, keepdims=True) + EPS) * gng
        yb = on * (gout * jax.nn.sigmoid(gout))
        mix_ref[r0:r0 + chunk, sc + hd * dv:sc + (hd + 1) * dv] = yb.astype(BF16)
    result.append(jnp.dot(mix_ref[...], wout_ref[...], preferred_element_type=F32))


def _even_exact_tile(x_ref, gpost_ref, gng_ref, wout_ref, mgpre_ref, x1buf_ref, hbuf_ref,
                     z_ref, mix_ref, bbuf_ref, sprev_ref, orow_ref, *, rows, chunk, d_model):
    sc = d_model // 2
    dk = d_model // 2 // GLA_HEADS
    dv = d_model // GLA_HEADS
    key = GLA_HEADS * dk
    o_q = 3 * sc
    o_k = o_q + key
    o_v = o_k + key
    o_g = o_v + GLA_HEADS * dv
    scale = dk ** -0.5
    tn_dims = (((0,), (0,)), ((), ()))
    gng = gng_ref[...]
    row_id = lax.broadcasted_iota(jnp.int32, (chunk, 1), 0)
    for hd in range(GLA_HEADS):
        klanes = slice(hd * dk, (hd + 1) * dk)
        qlanes = slice(o_q + hd * dk, o_q + (hd + 1) * dk)
        state = sprev_ref[hd]
        for c in range(rows // chunk):
            r0 = c * chunk
            bc = bbuf_ref[r0:r0 + chunk, klanes]
            kc = z_ref[r0:r0 + chunk, o_k + hd * dk:o_k + (hd + 1) * dk]
            vc = z_ref[r0:r0 + chunk, o_v + hd * dv:o_v + (hd + 1) * dv]
            gout = z_ref[r0:r0 + chunk, o_g + hd * dv:o_g + (hd + 1) * dv]
            b_last = bc[chunk - 1:chunk, :]
            q_in = (z_ref[r0:r0 + chunk, qlanes] * jnp.exp(bc) * scale).astype(BF16)
            k_end = (kc * jnp.exp(b_last - bc)).astype(BF16)

            def eight_rows(blk, carry):
                base = pl.multiple_of(blk * SUBLANES, SUBLANES)
                b_blk = bbuf_ref[pl.ds(r0 + base, SUBLANES), klanes]
                q_blk = z_ref[pl.ds(r0 + base, SUBLANES), qlanes] * scale
                out_rows = []
                for r in range(SUBLANES):
                    decay_ij = jnp.exp(jnp.minimum(b_blk[r:r + 1, :] - bc, 0.0))
                    w = (q_blk[r:r + 1, :] * decay_ij) * kc
                    a = jnp.where(row_id <= base + r, jnp.sum(w, axis=-1, keepdims=True), 0.0)
                    out_rows.append(jnp.sum(a * vc, axis=0, keepdims=True))
                orow_ref[pl.ds(base, SUBLANES), :] = jnp.concatenate(out_rows, axis=0)
                return carry

            lax.fori_loop(0, chunk // SUBLANES, eight_rows, 0)
            o = (jnp.dot(q_in, state.astype(BF16), preferred_element_type=F32) + orow_ref[...])
            decay = jnp.broadcast_to(jnp.exp(b_last), (dk, dk)).T
            state = (state * jnp.concatenate([decay] * (dv // dk), axis=1)
                     + lax.dot_general(k_end, vc.astype(BF16), tn_dims,
                                       preferred_element_type=F32))
            on = o * lax.rsqrt(jnp.mean(o * o, axis=-1, keepdims=True) + EPS) * gng
            yb = on * (gout * jax.nn.sigmoid(gout))
            mix_ref[r0:r0 + chunk, sc + hd * dv:sc + (hd + 1) * dv] = yb.astype(BF16)
    y = jnp.dot(mix_ref[...], wout_ref[...], preferred_element_type=F32)
    _store_mixer_output(x_ref[...] + _rms(y, gpost_ref[...]), mgpre_ref, x1buf_ref, hbuf_ref)


def _even_kernel(x_ref, gpre_ref, gpost_ref, win_ref, wgk2_ref, bgk2_ref, wsc_ref, gng_ref,
                 wout_hbm, mgpre_ref, mgpost_ref, w1_hbm, w2_hbm, o_ref,
                 x1buf_ref, hbuf_ref, z_ref, mix_ref, ubuf_ref, s_ref, bbuf_ref, sprev_ref,
                 orow_ref, wout_ref, w1_ref, w2_ref, stage_d_ref, stage_h_ref, sem_ref,
                 *, rows, chunk, d_model, layer, idx):
    @pl.when(pl.program_id(1) == 0)
    def _():
        x1buf_ref[...] = jnp.zeros_like(x1buf_ref)
        hbuf_ref[...] = jnp.zeros_like(hbuf_ref)
        s_ref[...] = jnp.zeros_like(s_ref)
        ubuf_ref[0:SUBLANES, :] = jnp.zeros((SUBLANES, ubuf_ref.shape[1]), F32)
        _load_weight_as_bf16(w1_hbm.at[layer], w1_ref, stage_h_ref, sem_ref)
        _load_weight_as_bf16(w2_hbm.at[layer], w2_ref, stage_d_ref, sem_ref)
        _load_weight_as_bf16(wout_hbm.at[idx], wout_ref, stage_d_ref, sem_ref)

    mixed, mlp_done = [], []
    _interleave(
        _mlp_stage(x1buf_ref, hbuf_ref[...], mgpost_ref, w1_ref, w2_ref, o_ref, mlp_done,
                   tail_parts=2),
        _even_mixer_stage(x_ref, gpre_ref, win_ref, wgk2_ref, bgk2_ref, wsc_ref, gng_ref,
                          wout_ref, z_ref, mix_ref, ubuf_ref, s_ref, bbuf_ref, sprev_ref, mixed,
                          rows=rows, chunk=chunk, d_model=d_model),
        head=EVEN_HEAD_ORDER)
    min_chunk_decay, y = mixed
    _store_mixer_output(x_ref[...] + _rms(y, gpost_ref[...]), mgpre_ref, x1buf_ref, hbuf_ref)

    @pl.when(min_chunk_decay < -GLA_FAST_PATH_MAX_DECAY)
    def _():
        _even_exact_tile(x_ref, gpost_ref, gng_ref, wout_ref, mgpre_ref, x1buf_ref, hbuf_ref,
                         z_ref, mix_ref, bbuf_ref, sprev_ref, orow_ref,
                         rows=rows, chunk=chunk, d_model=d_model)


def _tile_specs(rows, d, n_tiles):
    x_spec = pl.BlockSpec((None, rows, d), lambda b, s: (b, jnp.minimum(s, n_tiles - 1), 0))
    o_spec = pl.BlockSpec((None, rows, d), lambda b, s: (b, jnp.maximum(s - 1, 0), 0))
    return x_spec, o_spec


def _even_call(x3d, g_pre, g_post, w_in, w_gk2, b_gk2, w_sc, gn_g, w_out,
               mg_pre, mg_post, w1, w2, layer, idx):
    bsz, seq, d = x3d.shape
    rows, chunk = TILE_ROWS, GLA_CHUNK
    assert seq % rows == 0 and rows % chunk == 0
    n_tiles = seq // rows
    n_in = w_in.shape[-1]
    key = w_gk2.shape[-1]
    sc = w_sc.shape[-1]
    mix = w_out.shape[1]
    hidden = w1.shape[-1]
    dk = key // GLA_HEADS
    dv = d // GLA_HEADS
    x_spec, o_spec = _tile_specs(rows, d, n_tiles)
    hbm = pl.BlockSpec(memory_space=pl.ANY)
    kern = functools.partial(_even_kernel, rows=rows, chunk=chunk, d_model=d,
                             layer=layer, idx=idx)
    return pl.pallas_call(
        kern,
        grid=(bsz, n_tiles + 1),
        in_specs=[
            x_spec,
            _const_spec((None, 1, d), lambda b, s: (layer, 0, 0)),
            _const_spec((None, 1, d), lambda b, s: (layer, 0, 0)),
            _const_spec((None, d, n_in), lambda b, s: (idx, 0, 0)),
            _const_spec((None, LANES, key), lambda b, s: (idx, 0, 0)),
            _const_spec((None, 1, key), lambda b, s: (idx, 0, 0)),
            _const_spec((None, SC_KERNEL, sc), lambda b, s: (idx, 0, 0)),
            _const_spec((None, 1, dv), lambda b, s: (idx, 0, 0)),
            hbm,
            _const_spec((None, 1, d), lambda b, s: (layer, 0, 0)),
            _const_spec((None, 1, d), lambda b, s: (layer, 0, 0)),
            hbm,
            hbm,
        ],
        out_specs=o_spec,
        out_shape=jax.ShapeDtypeStruct((bsz, seq, d), F32),
        scratch_shapes=[
            pltpu.VMEM((rows, d), F32),
            pltpu.VMEM((rows, d), BF16),
            pltpu.VMEM((rows, n_in), F32),
            pltpu.VMEM((rows, mix), BF16),
            pltpu.VMEM((rows + SUBLANES, sc), F32),
            pltpu.VMEM((GLA_HEADS, dk, dv), F32),
            pltpu.VMEM((rows, key), F32),
            pltpu.VMEM((GLA_HEADS, dk, dv), F32),
            pltpu.VMEM((chunk, dv), F32),
            pltpu.VMEM((mix, d), BF16),
            pltpu.VMEM((d, hidden), BF16),
            pltpu.VMEM((hidden, d), BF16),
            _stage_scratch(d),
            _stage_scratch(hidden),
            pltpu.SemaphoreType.DMA((WEIGHT_STAGE_SLOTS,)),
        ],
        compiler_params=pltpu.CompilerParams(
            dimension_semantics=("arbitrary", "arbitrary"),
            vmem_limit_bytes=VMEM_LIMIT_BYTES),
        name="shortconv_gla_layer",
    )(x3d, g_pre, g_post, w_in, w_gk2, b_gk2, w_sc, gn_g, w_out, mg_pre, mg_post, w1, w2)


def _depthwise_conv(ubuf_ref, wdw_ref, bdw_ref, dbuf_ref, *, rows, d_model):
    sub = SUBLANES
    halo_blocks = CF_HALO // sub
    n_blocks = rows // sub
    max_a = (CF_KERNEL - 1) // sub
    assert max_a < halo_blocks
    row_id = lax.broadcasted_iota(jnp.int32, (sub, LANES), 0)
    for lg in range(d_model // LANES):
        lanes = slice(lg * LANES, (lg + 1) * LANES)
        w = {}
        for s in range(CF_KERNEL):
            w[s] = jnp.broadcast_to(wdw_ref[CF_KERNEL - 1 - s:CF_KERNEL - s, lanes], (sub, LANES))
        bias = jnp.broadcast_to(bdw_ref[:, lanes], (sub, LANES))
        blocks = {}

        def block(m):
            if m not in blocks:
                blocks[m] = ubuf_ref[m * sub:(m + 1) * sub, lanes]
            return blocks[m]

        prev_q = None
        for m in range(halo_blocks - 1, halo_blocks + n_blocks):
            q = []
            for r in range(sub):
                acc = None
                for a in range(max_a + 1):
                    s = sub * a + r
                    if s >= CF_KERNEL:
                        continue
                    term = w[s] * block(m - a)
                    acc = term if acc is None else acc + term
                q.append(acc)
            if m >= halo_blocks:
                out = bias + q[0]
                for r in range(1, sub):
                    both = jnp.where(row_id < sub - r, q[r], prev_q[r])
                    out = out + pltpu.roll(both, r, axis=0)
                k = m - halo_blocks
                dbuf_ref[k * sub:(k + 1) * sub, lanes] = out
            prev_q = q
            blocks.pop(m - max_a, None)
        yield


def _odd_kernel(x_ref, gpre_ref, gpost_ref, wpw1_hbm, bpw1_ref, wdw_ref, bdw_ref,
                lng_ref, lnb_ref, wpw2_hbm, bpw2_ref, mgpre_ref, mgpost_ref, w1_hbm, w2_hbm,
                o_ref, x1buf_ref, ubuf_ref, dbuf_ref, wpw1_ref, wpw2_ref, w1_ref, w2_ref,
                stage_d_ref, stage_2d_ref, stage_h_ref, sem_ref, *, rows, d_model, layer, idx):
    @pl.when(pl.program_id(1) == 0)
    def _():
        x1buf_ref[...] = jnp.zeros_like(x1buf_ref)
        ubuf_ref[0:CF_HALO, :] = jnp.zeros((CF_HALO, d_model), F32)
        _load_weight_as_bf16(wpw1_hbm.at[idx], wpw1_ref, stage_2d_ref, sem_ref)
        _load_weight_as_bf16(w1_hbm.at[layer], w1_ref, stage_h_ref, sem_ref)
        _load_weight_as_bf16(w2_hbm.at[layer], w2_ref, stage_d_ref, sem_ref)
        _load_weight_as_bf16(wpw2_hbm.at[idx], wpw2_ref, stage_d_ref, sem_ref)

    h = _rms(x_ref[...], gpre_ref[...]).astype(BF16)
    u = jnp.dot(h, wpw1_ref[...], preferred_element_type=F32) + bpw1_ref[...]
    ubuf_ref[CF_HALO:CF_HALO + rows, :] = u[:, :d_model] * jax.nn.sigmoid(u[:, d_model:])

    mlp_done = []
    h_mlp = _rms(x1buf_ref[...], mgpre_ref[...]).astype(BF16)
    _interleave(
        _mlp_stage(x1buf_ref, h_mlp, mgpost_ref, w1_ref, w2_ref, o_ref, mlp_done, tail_parts=1),
        _depthwise_conv(ubuf_ref, wdw_ref, bdw_ref, dbuf_ref, rows=rows, d_model=d_model))
    ubuf_ref[0:CF_HALO, :] = ubuf_ref[rows:rows + CF_HALO, :]

    _order_after(dbuf_ref, (slice(0, SUBLANES), slice(0, LANES)), mlp_done[-1])
    d = dbuf_ref[...]
    mu = jnp.mean(d, axis=-1, keepdims=True)
    dc = d - mu
    dn = dc * lax.rsqrt(jnp.mean(dc * dc, axis=-1, keepdims=True) + EPS)
    dn = dn * lng_ref[...] + lnb_ref[...]
    act = (dn * jax.nn.sigmoid(dn)).astype(BF16)
    y = jnp.dot(act, wpw2_ref[...], preferred_element_type=F32) + bpw2_ref[...]
    x1buf_ref[...] = x_ref[...] + _rms(y, gpost_ref[...])


def _odd_call(x3d, g_pre, g_post, w_pw1, b_pw1, w_dw, b_dw, ln_g, ln_b, w_pw2, b_pw2,
              mg_pre, mg_post, w1, w2, layer, idx):
    bsz, seq, d = x3d.shape
    rows = TILE_ROWS
    assert seq % rows == 0 and rows % SUBLANES == 0 and rows >= CF_HALO
    n_tiles = seq // rows
    hidden = w1.shape[-1]
    x_spec, o_spec = _tile_specs(rows, d, n_tiles)
    hbm = pl.BlockSpec(memory_space=pl.ANY)
    kern = functools.partial(_odd_kernel, rows=rows, d_model=d, layer=layer, idx=idx)
    return pl.pallas_call(
        kern,
        grid=(bsz, n_tiles + 1),
        in_specs=[
            x_spec,
            _const_spec((None, 1, d), lambda b, s: (layer, 0, 0)),
            _const_spec((None, 1, d), lambda b, s: (layer, 0, 0)),
            hbm,
            _const_spec((None, 1, 2 * d), lambda b, s: (idx, 0, 0)),
            _const_spec((None, CF_KERNEL, d), lambda b, s: (idx, 0, 0)),
            _const_spec((None, 1, d), lambda b, s: (idx, 0, 0)),
            _const_spec((None, 1, d), lambda b, s: (idx, 0, 0)),
            _const_spec((None, 1, d), lambda b, s: (idx, 0, 0)),
            hbm,
            _const_spec((None, 1, d), lambda b, s: (idx, 0, 0)),
            _const_spec((None, 1, d), lambda b, s: (layer, 0, 0)),
            _const_spec((None, 1, d), lambda b, s: (layer, 0, 0)),
            hbm,
            hbm,
        ],
        out_specs=o_spec,
        out_shape=jax.ShapeDtypeStruct((bsz, seq, d), F32),
        scratch_shapes=[
            pltpu.VMEM((rows, d), F32),
            pltpu.VMEM((rows + CF_HALO, d), F32),
            pltpu.VMEM((rows, d), F32),
            pltpu.VMEM((d, 2 * d), BF16),
            pltpu.VMEM((d, d), BF16),
            pltpu.VMEM((d, hidden), BF16),
            pltpu.VMEM((hidden, d), BF16),
            _stage_scratch(d),
            _stage_scratch(2 * d),
            _stage_scratch(hidden),
            pltpu.SemaphoreType.DMA((WEIGHT_STAGE_SLOTS,)),
        ],
        compiler_params=pltpu.CompilerParams(
            dimension_semantics=("arbitrary", "arbitrary"),
            vmem_limit_bytes=VMEM_LIMIT_BYTES),
        name="conformer_layer",
    )(x3d, g_pre, g_post, w_pw1, b_pw1, w_dw, b_dw, ln_g, ln_b, w_pw2, b_pw2,
      mg_pre, mg_post, w1, w2)


def kernel(x, norm_mix_pre, norm_mix_post, norm_mlp_pre, norm_mlp_post,
           ab_w_in, gla_w_gk2, gla_b_gk2, sc_w_conv, gla_norm_g, ab_w_out,
           cf_w_pw1, cf_b_pw1, cf_w_dw, cf_b_dw, cf_ln_g, cf_ln_b, cf_w_pw2, cf_b_pw2,
           mlp_w1, mlp_w2):
    depth = norm_mix_pre.shape[0]
    n_in = ab_w_in.shape[-1]
    n_main = n_in - GLA_RANK
    assert n_main % LANES == 0 and mlp_w1.shape[-1] % MLP_HIDDEN_CHUNK == 0

    n_pad = -(-(n_main + LANES) // MXU_DIM) * MXU_DIM
    w_in = jnp.pad(ab_w_in.astype(BF16), ((0, 0), (0, 0), (0, n_pad - n_in)))
    w_gk2 = jnp.pad(gla_w_gk2.astype(BF16), ((0, 0), (0, LANES - GLA_RANK), (0, 0)))
    w_out, w_pw1, w_pw2, w1, w2 = ab_w_out, cf_w_pw1, cf_w_pw2, mlp_w1, mlp_w2

    def rows3d(v):
        return v.reshape(v.shape[0], 1, v.shape[1])

    norm_mix_pre, norm_mix_post, norm_mlp_pre, norm_mlp_post = map(
        rows3d, (norm_mix_pre, norm_mix_post, norm_mlp_pre, norm_mlp_post))
    gla_b_gk2, gla_norm_g = rows3d(gla_b_gk2), rows3d(gla_norm_g)
    cf_b_pw1, cf_b_dw, cf_ln_g, cf_ln_b, cf_b_pw2 = map(
        rows3d, (cf_b_pw1, cf_b_dw, cf_ln_g, cf_ln_b, cf_b_pw2))

    for layer in range(depth):
        idx = layer // 2
        if layer % 2 == 0:
            x = _even_call(x, norm_mix_pre, norm_mix_post, w_in, w_gk2, gla_b_gk2,
                           sc_w_conv, gla_norm_g, w_out, norm_mlp_pre, norm_mlp_post,
                           w1, w2, layer, idx)
        else:
            x = _odd_call(x, norm_mix_pre, norm_mix_post, w_pw1, cf_b_pw1, cf_w_dw,
                          cf_b_dw, cf_ln_g, cf_ln_b, w_pw2, cf_b_pw2, norm_mlp_pre,
                          norm_mlp_post, w1, w2, layer, idx)
    return x
```

```python
import functools

import jax
import jax.numpy as jnp
from jax import lax
from jax.experimental import pallas as pl
from jax.experimental.pallas import tpu as pltpu

F32 = jnp.float32
BF16 = jnp.bfloat16

EPS = 1e-6
GLA_HEADS = 4
GLA_RANK = 16
GLA_GATE_NORM = 16.0
SC_KERNEL = 3
CF_KERNEL = 31

LANES = 128
SUBLANES = 8
MXU_DIM = 256
VMEM_LIMIT_BYTES = 56 * 1024 * 1024
WEIGHT_STAGE_BYTES = 1024 * 1024

TILE_ROWS = 256
MLP_HIDDEN_CHUNK = 1024
GLA_CHUNK = 128
EVEN_HEAD_ORDER = (0, 0, 0)
GLA_FAST_PATH_MAX_DECAY = 60.0
CF_HALO = 32


def _rms(x, g):
    ms = jnp.mean(x * x, axis=-1, keepdims=True)
    return x * lax.rsqrt(ms + EPS) * g


def _const_spec(shape, index_map):
    return pl.BlockSpec(shape, index_map, pipeline_mode=pl.Buffered(1))


def _zero_bits_of(v):
    u = lax.bitcast_convert_type(v, jnp.uint32)
    return lax.shift_right_logical(lax.shift_right_logical(u, jnp.uint32(16)), jnp.uint32(16))


def _order_after(ref, block, token):
    bits = lax.bitcast_convert_type(ref[block], jnp.uint32) | token
    ref[block] = lax.bitcast_convert_type(bits, F32)


def _load_weight_as_bf16(src_hbm_ref, dst_ref, stage_ref, sem_ref):
    k_rows, n_cols = dst_ref.shape
    chunk = stage_ref.shape[1]
    assert k_rows % chunk == 0 and stage_ref.shape[2] == n_cols
    n_chunks = k_rows // chunk

    def copy(i):
        return pltpu.make_async_copy(src_hbm_ref.at[pl.ds(i * chunk, chunk), :],
                                     stage_ref.at[i % 2], sem_ref.at[i % 2])

    copy(0).start()
    for i in range(n_chunks):
        if i + 1 < n_chunks:
            copy(i + 1).start()
        copy(i).wait()
        dst_ref[i * chunk:(i + 1) * chunk, :] = stage_ref[i % 2].astype(BF16)


def _stage_rows(n_cols):
    return WEIGHT_STAGE_BYTES // (4 * n_cols)


def _store_mixer_output(x1, mgpre_ref, x1buf_ref, hbuf_ref):
    x1buf_ref[...] = x1
    hbuf_ref[...] = _rms(x1, mgpre_ref[...]).astype(BF16)


def _mlp_stage(x1buf_ref, h, gpost_ref, w1_ref, w2_ref, o_ref, done, *, tail_parts):
    rows = h.shape[0]
    part = rows // tail_parts
    hidden = w1_ref.shape[1]
    n_chunks = hidden // MLP_HIDDEN_CHUNK

    def cols(c):
        return slice(c * MLP_HIDDEN_CHUNK, (c + 1) * MLP_HIDDEN_CHUNK)

    def up(c):
        raw = jnp.dot(h, w1_ref[:, cols(c)], preferred_element_type=F32)
        raw = jnp.maximum(raw, 0.0)
        return (raw * raw).astype(BF16)

    acc = None
    act = up(0)
    yield
    for c in range(1, n_chunks):
        act_next = up(c)
        yield
        p = jnp.dot(act, w2_ref[cols(c - 1), :], preferred_element_type=F32)
        acc = p if acc is None else acc + p
        act = act_next
        yield
    out = None
    for t in range(tail_parts):
        sl = slice(t * part, (t + 1) * part)
        p = jnp.dot(act[sl], w2_ref[cols(n_chunks - 1), :], preferred_element_type=F32)
        total = p if acc is None else acc[sl] + p
        out = x1buf_ref[sl, :] + _rms(total, gpost_ref[...])
        o_ref[sl, :] = out
    done.append(_zero_bits_of(out[0:SUBLANES, 0:LANES]))


def _interleave(*gens, head=()):
    live = list(gens)

    def step(g):
        try:
            next(g)
        except StopIteration:
            live.remove(g)

    for i in head:
        step(gens[i])
    while live:
        for g in list(live):
            step(g)


def _even_mixer_stage(x_ref, gpre_ref, win_ref, wgk2_ref, bgk2_ref, wsc_ref, gng_ref, wout_ref,
                      z_ref, mix_ref, ubuf_ref, s_ref, bbuf_ref, sprev_ref, result,
                      *, rows, chunk, d_model):
    sc = d_model // 2
    dk = d_model // 2 // GLA_HEADS
    dv = d_model // GLA_HEADS
    key = GLA_HEADS * dk
    o_scx, o_scb, o_scc = 0, sc, 2 * sc
    o_q = 3 * sc
    o_k = o_q + key
    o_v = o_k + key
    o_g = o_v + GLA_HEADS * dv
    o_lr = o_g + GLA_HEADS * dv

    h = _rms(x_ref[...], gpre_ref[...]).astype(BF16)
    z_ref[...] = jnp.dot(h, win_ref[...], preferred_element_type=F32)
    yield

    gk_lr = z_ref[:, o_lr:o_lr + LANES].astype(BF16)
    gk = jnp.dot(gk_lr, wgk2_ref[...], preferred_element_type=F32) + bgk2_ref[...]
    yield
    g = (jnp.minimum(gk, 0.0) - jnp.log1p(jnp.exp(-jnp.abs(gk)))) * (1.0 / GLA_GATE_NORM)
    g_hi = g.astype(BF16)
    g_lo = (g - g_hi.astype(F32)).astype(BF16)
    ri = lax.broadcasted_iota(jnp.int32, (rows, rows), 0)
    ci = lax.broadcasted_iota(jnp.int32, (rows, rows), 1)
    same_chunk = (ri // chunk) == (ci // chunk)
    tril_chunks = jnp.where(same_chunk & (ci <= ri), 1.0, 0.0).astype(BF16)
    b = (jnp.dot(tril_chunks, g_hi, preferred_element_type=F32)
         + jnp.dot(tril_chunks, g_lo, preferred_element_type=F32))
    bbuf_ref[...] = b
    sprev_ref[...] = s_ref[...]
    n_chunks = rows // chunk
    chunk_totals = jnp.concatenate(
        [b[(c + 1) * chunk - 1:(c + 1) * chunk, :] for c in range(n_chunks)], axis=0)
    result.append(jnp.min(chunk_totals))
    yield

    u = z_ref[:, o_scc:o_scc + sc] * z_ref[:, o_scx:o_scx + sc]
    ubuf_ref[SUBLANES:SUBLANES + rows, :] = u
    conv = wsc_ref[SC_KERNEL - 1:SC_KERNEL, :] * u
    for j in range(SC_KERNEL - 1):
        shift = SC_KERNEL - 1 - j
        conv = conv + wsc_ref[j:j + 1, :] * ubuf_ref[SUBLANES - shift:SUBLANES - shift + rows, :]
    ubuf_ref[0:SUBLANES, :] = u[rows - SUBLANES:rows, :]
    mix_ref[:, 0:sc] = (z_ref[:, o_scb:o_scb + sc] * conv).astype(BF16)

    causal = (lax.broadcasted_iota(jnp.int32, (chunk, chunk), 1)
              <= lax.broadcasted_iota(jnp.int32, (chunk, chunk), 0))
    scale = dk ** -0.5
    nt_dims = (((1,), (1,)), ((), ()))
    tn_dims = (((0,), (0,)), ((), ()))
    gng = gng_ref[...]
    pairs = [(hd, c) for hd in range(GLA_HEADS) for c in range(n_chunks)]

    q_in, v_bf, decay, attn, kv = {}, {}, {}, {}, {}
    for hd, c in pairs:
        r0 = c * chunk
        bc = b[r0:r0 + chunk, hd * dk:(hd + 1) * dk]
        qc = z_ref[r0:r0 + chunk, o_q + hd * dk:o_q + (hd + 1) * dk]
        kc = z_ref[r0:r0 + chunk, o_k + hd * dk:o_k + (hd + 1) * dk]
        v_bf[hd, c] = z_ref[r0:r0 + chunk, o_v + hd * dv:o_v + (hd + 1) * dv].astype(BF16)
        b_last = bc[chunk - 1:chunk, :]
        b_mid = bc[chunk // 2 - 1:chunk // 2, :]
        qd = qc * jnp.exp(bc) * scale
        q_mid = (qd * jnp.exp(-b_mid)).astype(BF16)
        k_mid = kc * jnp.exp(b_mid - bc)
        k_end = (kc * jnp.exp(b_last - bc)).astype(BF16)
        q_in[hd, c] = qd.astype(BF16)
        decay[hd, c] = jnp.broadcast_to(jnp.exp(b_last), (dk, dk)).T
        attn[hd, c] = lax.dot_general(q_mid, k_mid.astype(BF16), nt_dims,
                                      preferred_element_type=F32)
        kv[hd, c] = lax.dot_general(k_end, v_bf[hd, c], tn_dims, preferred_element_type=F32)
    yield

    outs = {}
    for hd in range(GLA_HEADS):
        state = s_ref[hd]
        for c in range(n_chunks):
            a = jnp.where(causal, attn[hd, c], 0.0).astype(BF16)
            outs[hd, c] = (jnp.dot(q_in[hd, c], state.astype(BF16), preferred_element_type=F32)
                           + jnp.dot(a, v_bf[hd, c], preferred_element_type=F32))
            state = (state * jnp.concatenate([decay[hd, c]] * (dv // dk), axis=1) + kv[hd, c])
        s_ref[hd] = state
    yield

    for hd, c in pairs:
        r0 = c * chunk
        o = outs[hd, c]
        gout = z_ref[r0:r0 + chunk, o_g + hd * dv:o_g + (hd + 1) * dv]
        on = o * lax.rsqrt(jnp.mean(o * o, axis=-1, keepdims=True) + EPS) * gng
        yb = on * (gout * jax.nn.sigmoid(gout))
        mix_ref[r0:r0 + chunk, sc + hd * dv:sc + (hd + 1) * dv] = yb.astype(BF16)
    result.append(jnp.dot(mix_ref[...], wout_ref[...], preferred_element_type=F32))


def _even_exact_tile(x_ref, gpost_ref, gng_ref, wout_ref, mgpre_ref, x1buf_ref, hbuf_ref,
                     z_ref, mix_ref, bbuf_ref, sprev_ref, orow_ref, *, rows, chunk, d_model):
    sc = d_model // 2
    dk = d_model // 2 // GLA_HEADS
    dv = d_model // GLA_HEADS
    key = GLA_HEADS * dk
    o_q = 3 * sc
    o_k = o_q + key
    o_v = o_k + key
    o_g = o_v + GLA_HEADS * dv
    scale = dk ** -0.5
    tn_dims = (((0,), (0,)), ((), ()))
    gng = gng_ref[...]
    row_id = lax.broadcasted_iota(jnp.int32, (chunk, 1), 0)
    for hd in range(GLA_HEADS):
        klanes = slice(hd * dk, (hd + 1) * dk)
        qlanes = slice(o_q + hd * dk, o_q + (hd + 1) * dk)
        state = sprev_ref[hd]
        for c in range(rows // chunk):
            r0 = c * chunk
            bc = bbuf_ref[r0:r0 + chunk, klanes]
            kc = z_ref[r0:r0 + chunk, o_k + hd * dk:o_k + (hd + 1) * dk]
            vc = z_ref[r0:r0 + chunk, o_v + hd * dv:o_v + (hd + 1) * dv]
            gout = z_ref[r0:r0 + chunk, o_g + hd * dv:o_g + (hd + 1) * dv]
            b_last = bc[chunk - 1:chunk, :]
            q_in = (z_ref[r0:r0 + chunk, qlanes] * jnp.exp(bc) * scale).astype(BF16)
            k_end = (kc * jnp.exp(b_last - bc)).astype(BF16)

            def eight_rows(blk, carry):
                base = pl.multiple_of(blk * SUBLANES, SUBLANES)
                b_blk = bbuf_ref[pl.ds(r0 + base, SUBLANES), klanes]
                q_blk = z_ref[pl.ds(r0 + base, SUBLANES), qlanes] * scale
                out_rows = []
                for r in range(SUBLANES):
                    decay_ij = jnp.exp(jnp.minimum(b_blk[r:r + 1, :] - bc, 0.0))
                    w = (q_blk[r:r + 1, :] * decay_ij) * kc
                    a = jnp.where(row_id <= base + r, jnp.sum(w, axis=-1, keepdims=True), 0.0)
                    out_rows.append(jnp.sum(a * vc, axis=0, keepdims=True))
                orow_ref[pl.ds(base, SUBLANES), :] = jnp.concatenate(out_rows, axis=0)
                return carry

            lax.fori_loop(0, chunk // SUBLANES, eight_rows, 0)
            o = (jnp.dot(q_in, state.astype(BF16), preferred_element_type=F32) + orow_ref[...])
            decay = jnp.broadcast_to(jnp.exp(b_last), (dk, dk)).T
            state = (state * jnp.concatenate([decay] * (dv // dk), axis=1)
                     + lax.dot_general(k_end, vc.astype(BF16), tn_dims,
                                       preferred_element_type=F32))
            on = o * lax.rsqrt(jnp.mean(o * o, axis=-1, keepdims=True) + EPS) * gng
            yb = on * (gout * jax.nn.sigmoid(gout))
            mix_ref[r0:r0 + chunk, sc + hd * dv:sc + (hd + 1) * dv] = yb.astype(BF16)
    y = jnp.dot(mix_ref[...], wout_ref[...], preferred_element_type=F32)
    _store_mixer_output(x_ref[...] + _rms(y, gpost_ref[...]), mgpre_ref, x1buf_ref, hbuf_ref)


def _even_kernel(x_ref, gpre_ref, gpost_ref, win_ref, wgk2_ref, bgk2_ref, wsc_ref, gng_ref,
                 wout_hbm, mgpre_ref, mgpost_ref, w1_hbm, w2_hbm, o_ref,
                 x1buf_ref, hbuf_ref, z_ref, mix_ref, ubuf_ref, s_ref, bbuf_ref, sprev_ref,
                 orow_ref, wout_ref, w1_ref, w2_ref, stage_d_ref, stage_h_ref, sem_ref,
                 *, rows, chunk, d_model, layer, idx):
    @pl.when(pl.program_id(1) == 0)
    def _():
        x1buf_ref[...] = jnp.zeros_like(x1buf_ref)
        hbuf_ref[...] = jnp.zeros_like(hbuf_ref)
        s_ref[...] = jnp.zeros_like(s_ref)
        ubuf_ref[0:SUBLANES, :] = jnp.zeros((SUBLANES, ubuf_ref.shape[1]), F32)
        _load_weight_as_bf16(w1_hbm.at[layer], w1_ref, stage_h_ref, sem_ref)
        _load_weight_as_bf16(w2_hbm.at[layer], w2_ref, stage_d_ref, sem_ref)
        _load_weight_as_bf16(wout_hbm.at[idx], wout_ref, stage_d_ref, sem_ref)

    mixed, mlp_done = [], []
    _interleave(
        _mlp_stage(x1buf_ref, hbuf_ref[...], mgpost_ref, w1_ref, w2_ref, o_ref, mlp_done,
                   tail_parts=2),
        _even_mixer_stage(x_ref, gpre_ref, win_ref, wgk2_ref, bgk2_ref, wsc_ref, gng_ref,
                          wout_ref, z_ref, mix_ref, ubuf_ref, s_ref, bbuf_ref, sprev_ref, mixed,
                          rows=rows, chunk=chunk, d_model=d_model),
        head=EVEN_HEAD_ORDER)
    min_chunk_decay, y = mixed
    _store_mixer_output(x_ref[...] + _rms(y, gpost_ref[...]), mgpre_ref, x1buf_ref, hbuf_ref)

    @pl.when(min_chunk_decay < -GLA_FAST_PATH_MAX_DECAY)
    def _():
        _even_exact_tile(x_ref, gpost_ref, gng_ref, wout_ref, mgpre_ref, x1buf_ref, hbuf_ref,
                         z_ref, mix_ref, bbuf_ref, sprev_ref, orow_ref,
                         rows=rows, chunk=chunk, d_model=d_model)


def _tile_specs(rows, d, n_tiles):
    x_spec = pl.BlockSpec((None, rows, d), lambda b, s: (b, jnp.minimum(s, n_tiles - 1), 0))
    o_spec = pl.BlockSpec((None, rows, d), lambda b, s: (b, jnp.maximum(s - 1, 0), 0))
    return x_spec, o_spec


def _even_call(x3d, g_pre, g_post, w_in, w_gk2, b_gk2, w_sc, gn_g, w_out,
               mg_pre, mg_post, w1, w2, layer, idx):
    bsz, seq, d = x3d.shape
    rows, chunk = TILE_ROWS, GLA_CHUNK
    assert seq % rows == 0 and rows % chunk == 0
    n_tiles = seq // rows
    n_in = w_in.shape[-1]
    key = w_gk2.shape[-1]
    sc = w_sc.shape[-1]
    mix = w_out.shape[1]
    hidden = w1.shape[-1]
    dk = key // GLA_HEADS
    dv = d // GLA_HEADS
    x_spec, o_spec = _tile_specs(rows, d, n_tiles)
    hbm = pl.BlockSpec(memory_space=pl.ANY)
    kern = functools.partial(_even_kernel, rows=rows, chunk=chunk, d_model=d,
                             layer=layer, idx=idx)
    return pl.pallas_call(
        kern,
        grid=(bsz, n_tiles + 1),
        in_specs=[
            x_spec,
            _const_spec((None, 1, d), lambda b, s: (layer, 0, 0)),
            _const_spec((None, 1, d), lambda b, s: (layer, 0, 0)),
            _const_spec((None, d, n_in), lambda b, s: (idx, 0, 0)),
            _const_spec((None, LANES, key), lambda b, s: (idx, 0, 0)),
            _const_spec((None, 1, key), lambda b, s: (idx, 0, 0)),
            _const_spec((None, SC_KERNEL, sc), lambda b, s: (idx, 0, 0)),
            _const_spec((None, 1, dv), lambda b, s: (idx, 0, 0)),
            hbm,
            _const_spec((None, 1, d), lambda b, s: (layer, 0, 0)),
            _const_spec((None, 1, d), lambda b, s: (layer, 0, 0)),
            hbm,
            hbm,
        ],
        out_specs=o_spec,
        out_shape=jax.ShapeDtypeStruct((bsz, seq, d), F32),
        scratch_shapes=[
            pltpu.VMEM((rows, d), F32),
            pltpu.VMEM((rows, d), BF16),
            pltpu.VMEM((rows, n_in), F32),
            pltpu.VMEM((rows, mix), BF16),
            pltpu.VMEM((rows + SUBLANES, sc), F32),
            pltpu.VMEM((GLA_HEADS, dk, dv), F32),
            pltpu.VMEM((rows, key), F32),
            pltpu.VMEM((GLA_HEADS, dk, dv), F32),
            pltpu.VMEM((chunk, dv), F32),
            pltpu.VMEM((mix, d), BF16),
            pltpu.VMEM((d, hidden), BF16),
            pltpu.VMEM((hidden, d), BF16),
            pltpu.VMEM((2, _stage_rows(d), d), F32),
            pltpu.VMEM((2, _stage_rows(hidden), hidden), F32),
            pltpu.SemaphoreType.DMA((2,)),
        ],
        compiler_params=pltpu.CompilerParams(
            dimension_semantics=("arbitrary", "arbitrary"),
            vmem_limit_bytes=VMEM_LIMIT_BYTES),
        name="shortconv_gla_layer",
    )(x3d, g_pre, g_post, w_in, w_gk2, b_gk2, w_sc, gn_g, w_out, mg_pre, mg_post, w1, w2)


def _depthwise_conv(ubuf_ref, wdw_ref, bdw_ref, dbuf_ref, *, rows, d_model):
    sub = SUBLANES
    halo_blocks = CF_HALO // sub
    n_blocks = rows // sub
    max_a = (CF_KERNEL - 1) // sub
    assert max_a < halo_blocks
    row_id = lax.broadcasted_iota(jnp.int32, (sub, LANES), 0)
    for lg in range(d_model // LANES):
        lanes = slice(lg * LANES, (lg + 1) * LANES)
        w = {}
        for s in range(CF_KERNEL):
            w[s] = jnp.broadcast_to(wdw_ref[CF_KERNEL - 1 - s:CF_KERNEL - s, lanes], (sub, LANES))
        bias = jnp.broadcast_to(bdw_ref[:, lanes], (sub, LANES))
        blocks = {}

        def block(m):
            if m not in blocks:
                blocks[m] = ubuf_ref[m * sub:(m + 1) * sub, lanes]
            return blocks[m]

        prev_q = None
        for m in range(halo_blocks - 1, halo_blocks + n_blocks):
            q = []
            for r in range(sub):
                acc = None
                for a in range(max_a + 1):
                    s = sub * a + r
                    if s >= CF_KERNEL:
                        continue
                    term = w[s] * block(m - a)
                    acc = term if acc is None else acc + term
                q.append(acc)
            if m >= halo_blocks:
                out = bias + q[0]
                for r in range(1, sub):
                    both = jnp.where(row_id < sub - r, q[r], prev_q[r])
                    out = out + pltpu.roll(both, r, axis=0)
                k = m - halo_blocks
                dbuf_ref[k * sub:(k + 1) * sub, lanes] = out
            prev_q = q
            blocks.pop(m - max_a, None)
        yield


def _odd_kernel(x_ref, gpre_ref, gpost_ref, wpw1_hbm, bpw1_ref, wdw_ref, bdw_ref,
                lng_ref, lnb_ref, wpw2_hbm, bpw2_ref, mgpre_ref, mgpost_ref, w1_hbm, w2_hbm,
                o_ref, x1buf_ref, ubuf_ref, dbuf_ref, wpw1_ref, wpw2_ref, w1_ref, w2_ref,
                stage_d_ref, stage_2d_ref, stage_h_ref, sem_ref, *, rows, d_model, layer, idx):
    @pl.when(pl.program_id(1) == 0)
    def _():
        x1buf_ref[...] = jnp.zeros_like(x1buf_ref)
        ubuf_ref[0:CF_HALO, :] = jnp.zeros((CF_HALO, d_model), F32)
        _load_weight_as_bf16(wpw1_hbm.at[idx], wpw1_ref, stage_2d_ref, sem_ref)
        _load_weight_as_bf16(w1_hbm.at[layer], w1_ref, stage_h_ref, sem_ref)
        _load_weight_as_bf16(w2_hbm.at[layer], w2_ref, stage_d_ref, sem_ref)
        _load_weight_as_bf16(wpw2_hbm.at[idx], wpw2_ref, stage_d_ref, sem_ref)

    h = _rms(x_ref[...], gpre_ref[...]).astype(BF16)
    u = jnp.dot(h, wpw1_ref[...], preferred_element_type=F32) + bpw1_ref[...]
    ubuf_ref[CF_HALO:CF_HALO + rows, :] = u[:, :d_model] * jax.nn.sigmoid(u[:, d_model:])

    mlp_done = []
    h_mlp = _rms(x1buf_ref[...], mgpre_ref[...]).astype(BF16)
    _interleave(
        _mlp_stage(x1buf_ref, h_mlp, mgpost_ref, w1_ref, w2_ref, o_ref, mlp_done, tail_parts=1),
        _depthwise_conv(ubuf_ref, wdw_ref, bdw_ref, dbuf_ref, rows=rows, d_model=d_model))
    ubuf_ref[0:CF_HALO, :] = ubuf_ref[rows:rows + CF_HALO, :]

    _order_after(dbuf_ref, (slice(0, SUBLANES), slice(0, LANES)), mlp_done[-1])
    d = dbuf_ref[...]
    mu = jnp.mean(d, axis=-1, keepdims=True)
    dc = d - mu
    dn = dc * lax.rsqrt(jnp.mean(dc * dc, axis=-1, keepdims=True) + EPS)
    dn = dn * lng_ref[...] + lnb_ref[...]
    act = (dn * jax.nn.sigmoid(dn)).astype(BF16)
    y = jnp.dot(act, wpw2_ref[...], preferred_element_type=F32) + bpw2_ref[...]
    x1buf_ref[...] = x_ref[...] + _rms(y, gpost_ref[...])


def _odd_call(x3d, g_pre, g_post, w_pw1, b_pw1, w_dw, b_dw, ln_g, ln_b, w_pw2, b_pw2,
              mg_pre, mg_post, w1, w2, layer, idx):
    bsz, seq, d = x3d.shape
    rows = TILE_ROWS
    assert seq % rows == 0 and rows % SUBLANES == 0 and rows >= CF_HALO
    n_tiles = seq // rows
    hidden = w1.shape[-1]
    x_spec, o_spec = _tile_specs(rows, d, n_tiles)
    hbm = pl.BlockSpec(memory_space=pl.ANY)
    kern = functools.partial(_odd_kernel, rows=rows, d_model=d, layer=layer, idx=idx)
    return pl.pallas_call(
        kern,
        grid=(bsz, n_tiles + 1),
        in_specs=[
            x_spec,
            _const_spec((None, 1, d), lambda b, s: (layer, 0, 0)),
            _const_spec((None, 1, d), lambda b, s: (layer, 0, 0)),
            hbm,
            _const_spec((None, 1, 2 * d), lambda b, s: (idx, 0, 0)),
            _const_spec((None, CF_KERNEL, d), lambda b, s: (idx, 0, 0)),
            _const_spec((None, 1, d), lambda b, s: (idx, 0, 0)),
            _const_spec((None, 1, d), lambda b, s: (idx, 0, 0)),
            _const_spec((None, 1, d), lambda b, s: (idx, 0, 0)),
            hbm,
            _const_spec((None, 1, d), lambda b, s: (idx, 0, 0)),
            _const_spec((None, 1, d), lambda b, s: (layer, 0, 0)),
            _const_spec((None, 1, d), lambda b, s: (layer, 0, 0)),
            hbm,
            hbm,
        ],
        out_specs=o_spec,
        out_shape=jax.ShapeDtypeStruct((bsz, seq, d), F32),
        scratch_shapes=[
            pltpu.VMEM((rows, d), F32),
            pltpu.VMEM((rows + CF_HALO, d), F32),
            pltpu.VMEM((rows, d), F32),
            pltpu.VMEM((d, 2 * d), BF16),
            pltpu.VMEM((d, d), BF16),
            pltpu.VMEM((d, hidden), BF16),
            pltpu.VMEM((hidden, d), BF16),
            pltpu.VMEM((2, _stage_rows(d), d), F32),
            pltpu.VMEM((2, _stage_rows(2 * d), 2 * d), F32),
            pltpu.VMEM((2, _stage_rows(hidden), hidden), F32),
            pltpu.SemaphoreType.DMA((2,)),
        ],
        compiler_params=pltpu.CompilerParams(
            dimension_semantics=("arbitrary", "arbitrary"),
            vmem_limit_bytes=VMEM_LIMIT_BYTES),
        name="conformer_layer",
    )(x3d, g_pre, g_post, w_pw1, b_pw1, w_dw, b_dw, ln_g, ln_b, w_pw2, b_pw2,
      mg_pre, mg_post, w1, w2)


def kernel(x, norm_mix_pre, norm_mix_post, norm_mlp_pre, norm_mlp_post,
           ab_w_in, gla_w_gk2, gla_b_gk2, sc_w_conv, gla_norm_g, ab_w_out,
           cf_w_pw1, cf_b_pw1, cf_w_dw, cf_b_dw, cf_ln_g, cf_ln_b, cf_w_pw2, cf_b_pw2,
           mlp_w1, mlp_w2):
    depth = norm_mix_pre.shape[0]
    n_in = ab_w_in.shape[-1]
    n_main = n_in - GLA_RANK
    assert n_main % LANES == 0 and mlp_w1.shape[-1] % MLP_HIDDEN_CHUNK == 0

    n_pad = -(-(n_main + LANES) // MXU_DIM) * MXU_DIM
    w_in = jnp.pad(ab_w_in.astype(BF16), ((0, 0), (0, 0), (0, n_pad - n_in)))
    w_gk2 = jnp.pad(gla_w_gk2.astype(BF16), ((0, 0), (0, LANES - GLA_RANK), (0, 0)))
    w_out, w_pw1, w_pw2, w1, w2 = ab_w_out, cf_w_pw1, cf_w_pw2, mlp_w1, mlp_w2

    def rows3d(v):
        return v.reshape(v.shape[0], 1, v.shape[1])

    norm_mix_pre, norm_mix_post, norm_mlp_pre, norm_mlp_post = map(
        rows3d, (norm_mix_pre, norm_mix_post, norm_mlp_pre, norm_mlp_post))
    gla_b_gk2, gla_norm_g = rows3d(gla_b_gk2), rows3d(gla_norm_g)
    cf_b_pw1, cf_b_dw, cf_ln_g, cf_ln_b, cf_b_pw2 = map(
        rows3d, (cf_b_pw1, cf_b_dw, cf_ln_g, cf_ln_b, cf_b_pw2))

    for layer in range(depth):
        idx = layer // 2
        if layer % 2 == 0:
            x = _even_call(x, norm_mix_pre, norm_mix_post, w_in, w_gk2, gla_b_gk2,
                           sc_w_conv, gla_norm_g, w_out, norm_mlp_pre, norm_mlp_post,
                           w1, w2, layer, idx)
        else:
            x = _odd_call(x, norm_mix_pre, norm_mix_post, w_pw1, cf_b_pw1, cf_w_dw,
                          cf_b_dw, cf_ln_g, cf_ln_b, w_pw2, cf_b_pw2, norm_mlp_pre,
                          norm_mlp_post, w1, w2, layer, idx)
    return x
```
